```python
import jax, jax.numpy as jnp
from jax import lax
import numpy as np

D_MODEL = 1024
BATCH = 4
SEQ = 8192
DEPTH = 4

N_META = 16
BLOCK = 128
PAD = BLOCK - N_META

HEAD_DIM = 64
N_Q_HEADS = 16
N_KV_HEADS = 4
Q_PER_KV = N_Q_HEADS // N_KV_HEADS
WINDOW = 128
ROPE_THETA = 10000.0
ATTN_WIDTH = N_Q_HEADS * HEAD_DIM
KV_WIDTH = N_KV_HEADS * HEAD_DIM

D_INNER = 2 * D_MODEL
SSD_HEAD_DIM = 64
SSD_HEADS = D_INNER // SSD_HEAD_DIM
SSD_GROUPS = 4
SSD_HEADS_PER_GROUP = SSD_HEADS // SSD_GROUPS
SSD_STATE = 128
CONV_WIDTH = 4
CONV_DIM = D_INNER + 2 * SSD_GROUPS * SSD_STATE

IN_SIZES = (D_MODEL, D_MODEL, ATTN_WIDTH, KV_WIDTH, KV_WIDTH, D_INNER, CONV_DIM, SSD_HEADS)
IN_WIDTH = sum(IN_SIZES)

D_FF = 2816
N_EXPERTS = 8
TOP_K = 2
D_FF_EXPERT = 3584
N_DENSE = (DEPTH + 1) // 2
N_MOE = DEPTH // 2

ALPHA = (2 * DEPTH) ** 0.25
BETA = (8 * DEPTH) ** -0.25
LN_EPS = 1e-5
RMS_EPS = 1e-5

kernel_name = "hybrid_swa_ssd_moe_deepnorm"


def layer_norm(x, g, b):
    xf = x.astype(jnp.float32)
    mu = jnp.mean(xf, axis=-1, keepdims=True)
    var = jnp.mean(jnp.square(xf - mu), axis=-1, keepdims=True)
    y = (xf - mu) * lax.rsqrt(var + LN_EPS) * g.astype(jnp.float32) + b.astype(jnp.float32)
    return y.astype(x.dtype)


def rope(x, cos, sin):
    c = cos.astype(x.dtype)[None, :, None, :]
    s = sin.astype(x.dtype)[None, :, None, :]
    x1, x2 = jnp.split(x, 2, axis=-1)
    return jnp.concatenate([x1 * c - x2 * s, x2 * c + x1 * s], axis=-1)


def sliding_window_attention(q, k, v, sinks):
    b, T = q.shape[0], q.shape[1]
    nb = T // BLOCK
    qb = q.reshape(b, nb, BLOCK, N_KV_HEADS, Q_PER_KV, HEAD_DIM)
    kb = k.reshape(b, nb, BLOCK, N_KV_HEADS, HEAD_DIM)
    vb = v.reshape(b, nb, BLOCK, N_KV_HEADS, HEAD_DIM)
    shift = ((0, 0), (1, 0), (0, 0), (0, 0), (0, 0))
    kk = jnp.concatenate([jnp.pad(kb[:, :-1], shift), kb], axis=2)
    vv = jnp.concatenate([jnp.pad(vb[:, :-1], shift), vb], axis=2)
    s = jnp.einsum("bnqgrd,bnkgd->bngrqk", qb, kk).astype(jnp.float32) * (HEAD_DIM ** -0.5)
    i = jnp.arange(BLOCK)[:, None]
    j = jnp.arange(2 * BLOCK)[None, :]
    rel = BLOCK + i - j
    band = (rel >= 0) & (rel < WINDOW)
    kpos = (jnp.arange(nb)[:, None] - 1) * BLOCK + jnp.arange(2 * BLOCK)[None, :]
    kvalid = kpos >= PAD
    mask = band[None] & kvalid[:, None, :]
    s = jnp.where(mask[None, :, None, None], s, -jnp.inf)
    sink = sinks.astype(jnp.float32).reshape(N_KV_HEADS, Q_PER_KV)[None, None, :, :, None, None]
    m = jnp.maximum(jnp.max(s, axis=-1, keepdims=True), sink)
    p = jnp.exp(s - m)
    probs = p / (jnp.sum(p, axis=-1, keepdims=True) + jnp.exp(sink - m))
    o = jnp.einsum("bngrqk,bnkgd->bnqgrd", probs.astype(v.dtype), vv)
    return o.reshape(b, T, ATTN_WIDTH)


def causal_depthwise_conv(x, w, bias):
    y = lax.conv_general_dilated(
        x, w[:, None, :], window_strides=(1,), padding=[(CONV_WIDTH - 1, 0)],
        dimension_numbers=("NWC", "WIO", "NWC"), feature_group_count=x.shape[-1])
    return y + bias


def ssd_chunked(x, dt, a_log, bm, cm, d_skip):
    b, T = x.shape[0], x.shape[1]
    nc = T // BLOCK
    dtype = x.dtype
    a = -jnp.exp(a_log.astype(jnp.float32))
    da = dt * a
    xc = (x * dt[..., None].astype(dtype)).reshape(b, nc, BLOCK, SSD_GROUPS, SSD_HEADS_PER_GROUP, SSD_HEAD_DIM)
    bc = bm.reshape(b, nc, BLOCK, SSD_GROUPS, SSD_STATE)
    cc = cm.reshape(b, nc, BLOCK, SSD_GROUPS, SSD_STATE)
    a_cum = jnp.cumsum(jnp.moveaxis(da.reshape(b, nc, BLOCK, SSD_GROUPS, SSD_HEADS_PER_GROUP), 2, -1), axis=-1)
    causal = jnp.tril(jnp.ones((BLOCK, BLOCK), dtype=bool))
    seg = a_cum[..., :, None] - a_cum[..., None, :]
    lmat = jnp.exp(jnp.where(causal, seg, -jnp.inf)).astype(dtype)
    cb = jnp.einsum("bclgn,bcsgn->bcgls", cc, bc)
    y_diag = jnp.einsum("bcgrls,bcsgrp->bclgrp", cb[:, :, :, None] * lmat, xc)
    decay_states = jnp.exp(a_cum[..., -1:] - a_cum).astype(dtype)
    states = jnp.einsum("bclgn,bcgrl,bclgrp->bcgrpn", bc, decay_states, xc)
    chunk_decay = jnp.exp(a_cum[..., -1]).astype(dtype)

    def step(h, inp):
        st, dec = inp
        return h * dec[..., None, None] + st, h

    h0 = jnp.zeros_like(states[:, 0])
    _, state_in = lax.scan(step, h0, (jnp.moveaxis(states, 1, 0), jnp.moveaxis(chunk_decay, 1, 0)))
    state_in = jnp.moveaxis(state_in, 0, 1)
    y_off = jnp.einsum("bclgn,bcgrpn,bcgrl->bclgrp", cc, state_in, jnp.exp(a_cum).astype(dtype))
    y = (y_diag + y_off).reshape(b, T, SSD_HEADS, SSD_HEAD_DIM) + x * d_skip.astype(dtype)[:, None]
    return y.astype(dtype)


def gated_rmsnorm(y, z, g):
    u = (y * jax.nn.silu(z)).astype(jnp.float32)
    ug = u.reshape(u.shape[:-1] + (SSD_GROUPS, D_INNER // SSD_GROUPS))
    ug = ug * lax.rsqrt(jnp.mean(jnp.square(ug), axis=-1, keepdims=True) + RMS_EPS)
    return (ug.reshape(u.shape) * g.astype(jnp.float32)).astype(y.dtype)


def hybrid_mixer(h, w_in, conv_w, conv_b, dt_bias, a_log, d_skip, ssd_norm_g, sinks,
                 w_attn_out, w_ssd_out, w_o, cos, sin, valid):
    b, T, _ = h.shape
    proj = h @ w_in
    g_a, g_s, q, k, v, z, xbc, dt_raw = jnp.split(proj, list(np.cumsum(IN_SIZES)[:-1]), axis=-1)
    q = rope(q.reshape(b, T, N_Q_HEADS, HEAD_DIM), cos, sin)
    k = rope(k.reshape(b, T, N_KV_HEADS, HEAD_DIM), cos, sin)
    attn = sliding_window_attention(q, k, v.reshape(b, T, N_KV_HEADS, HEAD_DIM), sinks)
    xbc = jax.nn.silu(causal_depthwise_conv(xbc * valid, conv_w, conv_b))
    xs, bm, cm = jnp.split(xbc, [D_INNER, D_INNER + SSD_GROUPS * SSD_STATE], axis=-1)
    xs = xs * valid
    dt = jax.nn.softplus(dt_raw.astype(jnp.float32) + dt_bias.astype(jnp.float32))
    y = ssd_chunked(xs.reshape(b, T, SSD_HEADS, SSD_HEAD_DIM), dt, a_log,
                    bm.reshape(b, T, SSD_GROUPS, SSD_STATE), cm.reshape(b, T, SSD_GROUPS, SSD_STATE), d_skip)
    y = gated_rmsnorm(y.reshape(b, T, D_INNER), z, ssd_norm_g)
    merged = jax.nn.sigmoid(g_a) * (attn @ w_attn_out) + jax.nn.sigmoid(g_s) * (y @ w_ssd_out)
    return merged @ w_o


def swiglu(h, wg, wu, wd):
    return (jax.nn.silu(h @ wg) * (h @ wu)) @ wd


def moe_swiglu(h, w_router, wg, wu, wd):
    b, T, d = h.shape
    hf = h.reshape(b * T, d)
    logits = (hf @ w_router).astype(jnp.float32)
    top_v, top_i = lax.top_k(logits, TOP_K)
    top_w = jax.nn.softmax(top_v, axis=-1)
    gate = jnp.sum(jax.nn.one_hot(top_i, N_EXPERTS, dtype=jnp.float32) * top_w[..., None], axis=1).astype(h.dtype)
    out = jnp.zeros_like(hf)
    for e in range(N_EXPERTS):
        out = out + gate[:, e:e + 1] * swiglu(hf, wg[e], wu[e], wd[e])
    return out.reshape(b, T, d)


def setup_inputs(seed: int = 0) -> dict:
    key = jax.random.key(seed)
    ks = jax.random.split(key, 32)
    nrm = lambda k, shape, scale: jax.random.normal(k, shape, jnp.float32) * scale
    dt0 = jnp.exp(jax.random.uniform(ks[5], (DEPTH, SSD_HEADS), jnp.float32, np.log(1e-3), np.log(1e-1)))
    return {
        "x": nrm(ks[0], (BATCH, SEQ, D_MODEL), 1.0),
        "meta_tokens": nrm(ks[1], (N_META, D_MODEL), 1.0),
        "ln_in_g": 1.0 + nrm(ks[2], (D_MODEL,), 0.02),
        "ln_in_b": nrm(ks[3], (D_MODEL,), 0.02),
        "w_in": nrm(ks[4], (DEPTH, D_MODEL, IN_WIDTH), D_MODEL ** -0.5),
        "conv_w": nrm(ks[6], (DEPTH, CONV_WIDTH, CONV_DIM), CONV_WIDTH ** -0.5),
        "conv_b": nrm(ks[7], (DEPTH, CONV_DIM), 0.02),
        "dt_bias": dt0 + jnp.log(-jnp.expm1(-dt0)),
        "a_log": jnp.log(jax.random.uniform(ks[8], (DEPTH, SSD_HEADS), jnp.float32, 1.0, 16.0)),
        "d_skip": 1.0 + nrm(ks[9], (DEPTH, SSD_HEADS), 0.1),
        "ssd_norm_g": 1.0 + nrm(ks[10], (DEPTH, D_INNER), 0.02),
        "sinks": nrm(ks[11], (DEPTH, N_Q_HEADS), 0.5),
        "w_attn_out": nrm(ks[12], (DEPTH, ATTN_WIDTH, D_MODEL), ATTN_WIDTH ** -0.5),
        "w_ssd_out": nrm(ks[13], (DEPTH, D_INNER, D_MODEL), D_INNER ** -0.5),
        "w_o": nrm(ks[14], (DEPTH, D_MODEL, D_MODEL), BETA * D_MODEL ** -0.5),
        "ln1_g": 1.0 + nrm(ks[15], (DEPTH, D_MODEL), 0.02),
        "ln1_b": nrm(ks[16], (DEPTH, D_MODEL), 0.02),
        "ffn_wg": nrm(ks[17], (N_DENSE, D_MODEL, D_FF), D_MODEL ** -0.5),
        "ffn_wu": nrm(ks[18], (N_DENSE, D_MODEL, D_FF), D_MODEL ** -0.5),
        "ffn_wd": nrm(ks[19], (N_DENSE, D_FF, D_MODEL), BETA * D_FF ** -0.5),
        "moe_router": nrm(ks[20], (N_MOE, D_MODEL, N_EXPERTS), D_MODEL ** -0.5),
        "moe_wg": nrm(ks[21], (N_MOE, N_EXPERTS, D_MODEL, D_FF_EXPERT), D_MODEL ** -0.5),
        "moe_wu": nrm(ks[22], (N_MOE, N_EXPERTS, D_MODEL, D_FF_EXPERT), D_MODEL ** -0.5),
        "moe_wd": nrm(ks[23], (N_MOE, N_EXPERTS, D_FF_EXPERT, D_MODEL), BETA * D_FF_EXPERT ** -0.5),
        "ln2_g": 1.0 + nrm(ks[24], (DEPTH, D_MODEL), 0.02),
        "ln2_b": nrm(ks[25], (DEPTH, D_MODEL), 0.02),
    }


def reference(x, meta_tokens, ln_in_g, ln_in_b, w_in, conv_w, conv_b, dt_bias, a_log, d_skip,
              ssd_norm_g, sinks, w_attn_out, w_ssd_out, w_o, ln1_g, ln1_b, ffn_wg, ffn_wu, ffn_wd,
              moe_router, moe_wg, moe_wu, moe_wd, ln2_g, ln2_b):
    b, s, d = x.shape
    T = s + BLOCK
    h = jnp.concatenate([jnp.zeros((b, PAD, d), x.dtype),
                         jnp.broadcast_to(meta_tokens.astype(x.dtype)[None], (b, N_META, d)), x], axis=1)
    h = layer_norm(h, ln_in_g, ln_in_b)
    pos = (jnp.arange(T) - PAD).astype(jnp.float32)
    inv_freq = ROPE_THETA ** (-jnp.arange(0, HEAD_DIM, 2, dtype=jnp.float32) / HEAD_DIM)
    ang = pos[:, None] * inv_freq[None, :]
    cos, sin = jnp.cos(ang), jnp.sin(ang)
    valid = (jnp.arange(T) >= PAD).astype(x.dtype)[None, :, None]
    for l in range(DEPTH):
        mix = hybrid_mixer(h, w_in[l], conv_w[l], conv_b[l], dt_bias[l], a_log[l], d_skip[l],
                           ssd_norm_g[l], sinks[l], w_attn_out[l], w_ssd_out[l], w_o[l], cos, sin, valid)
        h = layer_norm(ALPHA * h + mix, ln1_g[l], ln1_b[l])
        if l % 2 == 0:
            f = swiglu(h, ffn_wg[l // 2], ffn_wu[l // 2], ffn_wd[l // 2])
        else:
            f = moe_swiglu(h, moe_router[l // 2], moe_wg[l // 2], moe_wu[l // 2], moe_wd[l // 2])
        h = layer_norm(ALPHA * h + f, ln2_g[l], ln2_b[l])
    return h[:, BLOCK:]
```

```python
import functools

import jax
import jax.numpy as jnp
from jax import lax
from jax.experimental import pallas as pl
from jax.experimental.pallas import tpu as pltpu

D_MODEL = 1024
DEPTH = 4
N_META = 16
BLOCK = 128
PAD = BLOCK - N_META
HEAD_DIM = 64
N_Q_HEADS = 16
N_KV_HEADS = 4
ROPE_THETA = 10000.0
ATTN_WIDTH = N_Q_HEADS * HEAD_DIM
KV_WIDTH = N_KV_HEADS * HEAD_DIM
D_INNER = 2 * D_MODEL
SSD_HEAD_DIM = 64
SSD_HEADS = D_INNER // SSD_HEAD_DIM
SSD_GROUPS = 4
SSD_STATE = 128
GROUP_WIDTH = D_INNER // SSD_GROUPS
CONV_WIDTH = 4
BC_WIDTH = SSD_GROUPS * SSD_STATE
D_FF = 2816
N_EXPERTS = 8
D_FF_EXPERT = 3584
ALPHA = (2 * DEPTH) ** 0.25
LN_EPS = 1e-5
RMS_EPS = 1e-5

LANES = 128
NEG_BIG = -1e30
VMEM_LIMIT = 56 * 1024 * 1024

OFF_Z = 0
OFF_XS = OFF_Z + D_INNER
OFF_GA = OFF_XS + D_INNER
OFF_GS = OFF_GA + D_MODEL
OFF_Q = OFF_GS + D_MODEL
OFF_B = OFF_Q + ATTN_WIDTH
OFF_C = OFF_B + BC_WIDTH
OFF_K = OFF_C + BC_WIDTH
OFF_V = OFF_K + KV_WIDTH
OFF_DT = OFF_V + KV_WIDTH
PROJ_TILE = 1024
PROJ_WIDTH = ((OFF_DT + LANES + PROJ_TILE - 1) // PROJ_TILE) * PROJ_TILE

F32 = jnp.float32
BF16 = jnp.bfloat16


def _pick_tile(n, candidates):
    for c in candidates:
        if n % c == 0:
            return c
    raise ValueError(f"no tile in {candidates} divides {n}")


def _params(*semantics):
    return pltpu.CompilerParams(dimension_semantics=semantics, vmem_limit_bytes=VMEM_LIMIT)


def _layer_norm(x, g, b):
    mu = jnp.mean(x, axis=-1, keepdims=True)
    xc = x - mu
    var = jnp.mean(xc * xc, axis=-1, keepdims=True)
    return xc * lax.rsqrt(var + LN_EPS) * g + b


def _silu(x):
    return x * jax.nn.sigmoid(x)


def _dot(a, b):
    return jnp.dot(a, b, preferred_element_type=F32)


def _dot_nt(a, b):
    return lax.dot_general(a, b, (((1,), (1,)), ((), ())), preferred_element_type=F32)


def _split2(x):
    hi = x.astype(BF16)
    lo = (x - hi.astype(F32)).astype(BF16)
    return hi, lo


def _dot_exact_rhs(x, m):
    hi, lo = _split2(x)
    return _dot(hi, m) + _dot(lo, m)


def _ln_kernel(x_ref, g_ref, b_ref, o32_ref, o16_ref):
    y = _layer_norm(x_ref[...], g_ref[...], b_ref[...])
    o32_ref[...] = y
    o16_ref[...] = y.astype(BF16)


def _ln_call(x, g, b):
    n, d = x.shape
    tm = _pick_tile(n, (1280, 640, 512, 256, 128))
    row = pl.BlockSpec((tm, d), lambda i: (i, 0))
    vec = pl.BlockSpec((1, d), lambda i: (0, 0))
    return pl.pallas_call(
        _ln_kernel,
        grid=(n // tm,),
        in_specs=[row, vec, vec],
        out_specs=[row, row],
        out_shape=[jax.ShapeDtypeStruct((n, d), F32), jax.ShapeDtypeStruct((n, d), BF16)],
        compiler_params=_params("parallel"),
        name="ln_in",
    )(x, g.reshape(1, d), b.reshape(1, d))


def _proj_kernel(x_ref, w_ref, o_ref):
    o_ref[...] = _dot(x_ref[...], w_ref[...])


def _proj_call(hb, w):
    n, d = hb.shape
    width = w.shape[1]
    tm = _pick_tile(n, (1280, 640, 512, 256, 128))
    tn = PROJ_TILE
    return pl.pallas_call(
        _proj_kernel,
        grid=(n // tm, width // tn),
        in_specs=[pl.BlockSpec((tm, d), lambda i, j: (i, 0)), pl.BlockSpec((d, tn), lambda i, j: (0, j))],
        out_specs=pl.BlockSpec((tm, tn), lambda i, j: (i, j)),
        out_shape=jax.ShapeDtypeStruct((n, width), F32),
        compiler_params=_params("parallel", "arbitrary"),
        name="in_proj",
    )(hb, w)


def _rope(x, cos, sin_signed):
    width = x.shape[1]
    reps = width // LANES
    c = jnp.tile(cos, (1, reps))
    s = jnp.tile(sin_signed, (1, reps))
    lane = lax.broadcasted_iota(jnp.int32, x.shape, 1)
    first_half = (lane % HEAD_DIM) < (HEAD_DIM // 2)
    partner = jnp.where(first_half, pltpu.roll(x, width - HEAD_DIM // 2, 1), pltpu.roll(x, HEAD_DIM // 2, 1))
    return x * c + partner * s


def _half_split(t):
    lane = lax.broadcasted_iota(jnp.int32, t.shape, 1)
    low = lane < HEAD_DIM
    swapped = pltpu.roll(t, HEAD_DIM, 1)
    zero = jnp.zeros_like(t)
    a_lo = jnp.where(low, t, zero).astype(BF16)
    a_hi = jnp.where(low, zero, swapped).astype(BF16)
    b_lo = jnp.where(low, swapped, zero).astype(BF16)
    b_hi = jnp.where(low, zero, t).astype(BF16)
    return (a_lo, a_hi), (b_lo, b_hi)


def _attn_kernel(sink_ref, q_ref, kc_ref, kp_ref, vc_ref, vp_ref, cc_ref, sc_ref, cp_ref, sp_ref, o_ref):
    n = pl.program_id(1)
    cos_c, sin_c = cc_ref[...], sc_ref[...]
    q = _rope(q_ref[...], cos_c, sin_c)
    k = jnp.concatenate([_rope(kp_ref[...], cp_ref[...], sp_ref[...]), _rope(kc_ref[...], cos_c, sin_c)], axis=0)
    v = jnp.concatenate([vp_ref[...], vc_ref[...]], axis=0)

    i = lax.broadcasted_iota(jnp.int32, (BLOCK, 2 * BLOCK), 0)
    jj = lax.broadcasted_iota(jnp.int32, (BLOCK, 2 * BLOCK), 1)
    visible = (jj > i) & (jj <= i + BLOCK) & (jj + (n - 1) * BLOCK >= PAD)
    bias = jnp.where(visible, 0.0, NEG_BIG).astype(F32)

    k_ops, v_ops = [], []
    for t in range(KV_WIDTH // LANES):
        ka, kb = _half_split(k[:, t * LANES:(t + 1) * LANES])
        va, vb = _half_split(v[:, t * LANES:(t + 1) * LANES])
        k_ops += [ka, kb]
        v_ops += [va, vb]

    heads_per_kv = N_Q_HEADS // N_KV_HEADS
    for p in range(ATTN_WIDTH // LANES):
        g = (2 * p) // heads_per_kv
        qp = q[:, p * LANES:(p + 1) * LANES].astype(BF16)
        out = jnp.zeros((BLOCK, LANES), F32)
        for half in range(2):
            sink = sink_ref[2 * p + half]
            s = _dot_nt(qp, k_ops[g][half]) + bias
            m = jnp.maximum(jnp.max(s, axis=-1, keepdims=True), sink)
            e = jnp.exp(s - m)
            denom = jnp.sum(e, axis=-1, keepdims=True) + jnp.exp(sink - m)
            out = out + _dot(e.astype(BF16), v_ops[g][half]) / denom
        o_ref[:, p * LANES:(p + 1) * LANES] = out.astype(o_ref.dtype)


def _attn_call(proj, sinks, cos_t, sin_t, batch, nb):
    n = proj.shape[0]

    def row(b, i):
        return b * nb + i

    def prev(b, i):
        return b * nb + jnp.maximum(i - 1, 0)

    kv_blk = OFF_K // KV_WIDTH
    in_specs = [
        pl.BlockSpec(memory_space=pltpu.SMEM),
        pl.BlockSpec((BLOCK, ATTN_WIDTH), lambda b, i: (row(b, i), OFF_Q // ATTN_WIDTH)),
        pl.BlockSpec((BLOCK, KV_WIDTH), lambda b, i: (row(b, i), kv_blk)),
        pl.BlockSpec((BLOCK, KV_WIDTH), lambda b, i: (prev(b, i), kv_blk)),
        pl.BlockSpec((BLOCK, KV_WIDTH), lambda b, i: (row(b, i), kv_blk + 1)),
        pl.BlockSpec((BLOCK, KV_WIDTH), lambda b, i: (prev(b, i), kv_blk + 1)),
        pl.BlockSpec((BLOCK, LANES), lambda b, i: (i, 0)),
        pl.BlockSpec((BLOCK, LANES), lambda b, i: (i, 0)),
        pl.BlockSpec((BLOCK, LANES), lambda b, i: (jnp.maximum(i - 1, 0), 0)),
        pl.BlockSpec((BLOCK, LANES), lambda b, i: (jnp.maximum(i - 1, 0), 0)),
    ]
    return pl.pallas_call(
        _attn_kernel,
        grid=(batch, nb),
        in_specs=in_specs,
        out_specs=pl.BlockSpec((BLOCK, ATTN_WIDTH), lambda b, i: (row(b, i), 0)),
        out_shape=jax.ShapeDtypeStruct((n, ATTN_WIDTH), BF16),
        compiler_params=_params("parallel", "arbitrary"),
        name="swa_attn",
    )(sinks, proj, proj, proj, proj, proj, cos_t, sin_t, cos_t, sin_t)


CARRY = 8


def _causal_conv(buf_ref, raw, w_ref, b_ref):
    buf_ref[CARRY:CARRY + BLOCK, :] = raw
    acc = b_ref[...] + w_ref[CONV_WIDTH - 1:CONV_WIDTH, :] * raw
    for j in range(CONV_WIDTH - 1):
        start = CARRY - (CONV_WIDTH - 1) + j
        acc = acc + w_ref[j:j + 1, :] * buf_ref[start:start + BLOCK, :]
    buf_ref[0:CARRY, :] = raw[BLOCK - CARRY:, :]
    return acc


def _ssd_kernel(xs_ref, z_ref, b_ref, c_ref, dt_ref, cwx_ref, cwb_ref, cwc_ref, cbx_ref, cbb_ref, cbc_ref,
                dtb_ref, alog_ref, dskip_ref, ng_ref, exp_ref, o_ref,
                bufx, bufb, bufc, state, y_scr):
    c = pl.program_id(1)

    @pl.when(c == 0)
    def _():
        state[...] = jnp.zeros_like(state)
        bufx[0:CARRY, :] = jnp.zeros((CARRY, D_INNER), F32)
        bufb[0:CARRY, :] = jnp.zeros((CARRY, BC_WIDTH), F32)
        bufc[0:CARRY, :] = jnp.zeros((CARRY, BC_WIDTH), F32)

    row = lax.broadcasted_iota(jnp.int32, (BLOCK, 1), 0)
    valid = ((row + c * BLOCK) >= PAD).astype(F32)

    xs = _silu(_causal_conv(bufx, xs_ref[...] * valid, cwx_ref, cbx_ref)) * valid
    bm = _silu(_causal_conv(bufb, b_ref[...] * valid, cwb_ref, cbb_ref)).astype(BF16)
    cm = _silu(_causal_conv(bufc, c_ref[...] * valid, cwc_ref, cbc_ref)).astype(BF16)

    dt = jnp.logaddexp(dt_ref[...] + dtb_ref[...], 0.0)
    da = dt * (-jnp.exp(alog_ref[...]))
    li = lax.broadcasted_iota(jnp.int32, (BLOCK, BLOCK), 0)
    si = lax.broadcasted_iota(jnp.int32, (BLOCK, BLOCK), 1)
    causal = li >= si
    tril = jnp.where(causal, 1.0, 0.0).astype(BF16)
    hi = da.astype(BF16)
    r1 = da - hi.astype(F32)
    mid = r1.astype(BF16)
    lo = (r1 - mid.astype(F32)).astype(BF16)
    a_cum = _dot(tril, hi) + _dot(tril, mid) + _dot(tril, lo)
    a_cum_t = a_cum.T
    a_last = a_cum[BLOCK - 1:BLOCK, :]
    exp_a = jnp.exp(a_cum)
    decay_to_end = jnp.exp(a_last - a_cum)

    expand = exp_ref[...]
    dt_x = _dot_exact_rhs(dt, expand)
    exp_a_x = _dot_exact_rhs(exp_a, expand)
    decay_x = _dot_exact_rhs(decay_to_end, expand)
    chunk_decay_x = exp_a_x[BLOCK - 1:BLOCK, :]

    xdt = xs * dt_x
    xdec_b = (xdt * decay_x).astype(BF16)

    lane = lax.broadcasted_iota(jnp.int32, (BLOCK, LANES), 1)
    low = lane < SSD_HEAD_DIM
    heads_per_group = SSD_HEADS // SSD_GROUPS
    for g in range(SSD_GROUPS):
        gs = slice(g * GROUP_WIDTH, (g + 1) * GROUP_WIDTH)
        bg = bm[:, g * SSD_STATE:(g + 1) * SSD_STATE]
        cg = cm[:, g * SSD_STATE:(g + 1) * SSD_STATE]
        cb = _dot_nt(cg, bg)
        bg_t = bg.astype(F32).T.astype(BF16)
        new_states = _dot(bg_t, xdec_b[:, gs])
        st = state[:, gs]
        y_off = _dot(cg, st.astype(BF16)) * exp_a_x[:, gs]
        state[:, gs] = st * chunk_decay_x[:, gs] + new_states
        for j in range(heads_per_group // 2):
            h0 = g * heads_per_group + 2 * j
            ms = []
            for h in (h0, h0 + 1):
                seg = a_cum[:, h:h + 1] - a_cum_t[h:h + 1, :]
                ms.append((cb * jnp.exp(jnp.where(causal, seg, NEG_BIG))).astype(BF16))
            lhs = jnp.concatenate(ms, axis=1)
            xp = xdt[:, h0 * SSD_HEAD_DIM:h0 * SSD_HEAD_DIM + LANES]
            zero = jnp.zeros_like(xp)
            rhs = jnp.concatenate([jnp.where(low, xp, zero), jnp.where(low, zero, xp)], axis=0).astype(BF16)
            y_scr[:, h0 * SSD_HEAD_DIM:h0 * SSD_HEAD_DIM + LANES] = (
                _dot(lhs, rhs) + y_off[:, 2 * j * SSD_HEAD_DIM:2 * j * SSD_HEAD_DIM + LANES])

    y = y_scr[...] + xs * dskip_ref[...]
    u = y * _silu(z_ref[...])
    for g in range(SSD_GROUPS):
        gs = slice(g * GROUP_WIDTH, (g + 1) * GROUP_WIDTH)
        ug = u[:, gs]
        scale = lax.rsqrt(jnp.mean(ug * ug, axis=-1, keepdims=True) + RMS_EPS)
        o_ref[:, gs] = (ug * scale * ng_ref[:, gs]).astype(o_ref.dtype)


def _ssd_call(proj, conv_w, conv_b, dt_bias, a_log, d_skip, norm_g, expand, batch, nc):
    n = proj.shape[0]

    def row(b, i):
        return b * nc + i

    def full(shape):
        return pl.BlockSpec(shape, lambda b, i: (0,) * len(shape))

    cwx, cwb, cwc = conv_w[:, :D_INNER], conv_w[:, D_INNER:D_INNER + BC_WIDTH], conv_w[:, D_INNER + BC_WIDTH:]
    cbx = conv_b[None, :D_INNER]
    cbb = conv_b[None, D_INNER:D_INNER + BC_WIDTH]
    cbc = conv_b[None, D_INNER + BC_WIDTH:]
    pad = LANES - SSD_HEADS
    dtb = jnp.pad(dt_bias, (0, pad))[None, :]
    alog = jnp.pad(a_log, (0, pad))[None, :]
    dskip = jnp.repeat(d_skip, SSD_HEAD_DIM)[None, :]
    in_specs = [
        pl.BlockSpec((BLOCK, D_INNER), lambda b, i: (row(b, i), OFF_XS // D_INNER)),
        pl.BlockSpec((BLOCK, D_INNER), lambda b, i: (row(b, i), OFF_Z // D_INNER)),
        pl.BlockSpec((BLOCK, BC_WIDTH), lambda b, i: (row(b, i), OFF_B // BC_WIDTH)),
        pl.BlockSpec((BLOCK, BC_WIDTH), lambda b, i: (row(b, i), OFF_C // BC_WIDTH)),
        pl.BlockSpec((BLOCK, LANES), lambda b, i: (row(b, i), OFF_DT // LANES)),
        full((CONV_WIDTH, D_INNER)), full((CONV_WIDTH, BC_WIDTH)), full((CONV_WIDTH, BC_WIDTH)),
        full((1, D_INNER)), full((1, BC_WIDTH)), full((1, BC_WIDTH)),
        full((1, LANES)), full((1, LANES)), full((1, D_INNER)), full((1, D_INNER)),
        full((LANES, D_INNER)),
    ]
    scratch = [
        pltpu.VMEM((CARRY + BLOCK, D_INNER), F32),
        pltpu.VMEM((CARRY + BLOCK, BC_WIDTH), F32),
        pltpu.VMEM((CARRY + BLOCK, BC_WIDTH), F32),
        pltpu.VMEM((SSD_STATE, D_INNER), F32),
        pltpu.VMEM((BLOCK, D_INNER), F32),
    ]
    return pl.pallas_call(
        _ssd_kernel,
        grid=(batch, nc),
        in_specs=in_specs,
        out_specs=pl.BlockSpec((BLOCK, D_INNER), lambda b, i: (row(b, i), 0)),
        out_shape=jax.ShapeDtypeStruct((n, D_INNER), BF16),
        scratch_shapes=scratch,
        compiler_params=_params("arbitrary", "arbitrary"),
        name="ssd",
    )(proj, proj, proj, proj, proj, cwx, cwb, cwc, cbx, cbb, cbc, dtb, alog, dskip, norm_g[None, :], expand)


def _merge_kernel(attn_ref, y_ref, ga_ref, gs_ref, h_ref, wa_ref, ws_ref, wo_ref, g_ref, b_ref, o32_ref, o16_ref):
    a = _dot(attn_ref[...], wa_ref[...])
    s = _dot(y_ref[...], ws_ref[...])
    merged = jax.nn.sigmoid(ga_ref[...]) * a + jax.nn.sigmoid(gs_ref[...]) * s
    mix = _dot(merged.astype(BF16), wo_ref[...])
    y = _layer_norm(ALPHA * h_ref[...] + mix, g_ref[...], b_ref[...])
    o32_ref[...] = y
    o16_ref[...] = y.astype(BF16)


def _merge_call(attn, yn, proj, h32, wa, ws, wo, g, b):
    n, d = h32.shape
    tm = _pick_tile(n, (640, 512, 256, 128))
    row = pl.BlockSpec((tm, d), lambda i: (i, 0))
    vec = pl.BlockSpec((1, d), lambda i: (0, 0))

    def const(shape):
        return pl.BlockSpec(shape, lambda i: (0, 0), pipeline_mode=pl.Buffered(1))

    in_specs = [
        pl.BlockSpec((tm, ATTN_WIDTH), lambda i: (i, 0)),
        pl.BlockSpec((tm, D_INNER), lambda i: (i, 0)),
        pl.BlockSpec((tm, D_MODEL), lambda i: (i, OFF_GA // D_MODEL)),
        pl.BlockSpec((tm, D_MODEL), lambda i: (i, OFF_GS // D_MODEL)),
        row,
        const((ATTN_WIDTH, d)), const((D_INNER, d)), const((d, d)),
        vec, vec,
    ]
    return pl.pallas_call(
        _merge_kernel,
        grid=(n // tm,),
        in_specs=in_specs,
        out_specs=[row, row],
        out_shape=[jax.ShapeDtypeStruct((n, d), F32), jax.ShapeDtypeStruct((n, d), BF16)],
        compiler_params=_params("parallel"),
        name="merge_out",
    )(attn, yn, proj, proj, h32, wa, ws, wo, g.reshape(1, d), b.reshape(1, d))


def _ffn_kernel(*refs, gated):
    if gated:
        gate_ref, hb_ref, h_ref, wg_ref, wu_ref, wd_ref, g_ref, b_ref, o32_ref, o16_ref, acc = refs
    else:
        hb_ref, h_ref, wg_ref, wu_ref, wd_ref, g_ref, b_ref, o32_ref, o16_ref, acc = refs
    e, j = pl.program_id(1), pl.program_id(2)

    @pl.when((e == 0) & (j == 0))
    def _():
        acc[...] = jnp.zeros_like(acc)

    hb = hb_ref[...]
    act = (_silu(_dot(hb, wg_ref[...])) * _dot(hb, wu_ref[...])).astype(BF16)
    part = _dot(act, wd_ref[...])
    if gated:
        lane = lax.broadcasted_iota(jnp.int32, gate_ref.shape, 1)
        gate = jnp.sum(jnp.where(lane == e, gate_ref[...], 0.0), axis=-1, keepdims=True)
        part = part * gate
    acc[...] += part

    @pl.when((e == pl.num_programs(1) - 1) & (j == pl.num_programs(2) - 1))
    def _():
        y = _layer_norm(ALPHA * h_ref[...] + acc[...], g_ref[...], b_ref[...])
        o32_ref[...] = y
        o16_ref[...] = y.astype(BF16)


def _ffn_call(hb, h32, wg, wu, wd, g, b, gates=None):
    n, d = h32.shape
    n_exp, _, f = wg.shape
    tm = _pick_tile(n, (640, 512, 256, 128))
    tf = _pick_tile(f, (1408, 896, 512, 256, 128))
    row = pl.BlockSpec((tm, d), lambda i, e, j: (i, 0))
    vec = pl.BlockSpec((1, d), lambda i, e, j: (0, 0))
    in_specs = [
        row, row,
        pl.BlockSpec((None, d, tf), lambda i, e, j: (e, 0, j)),
        pl.BlockSpec((None, d, tf), lambda i, e, j: (e, 0, j)),
        pl.BlockSpec((None, tf, d), lambda i, e, j: (e, j, 0)),
        vec, vec,
    ]
    args = [hb, h32, wg, wu, wd, g.reshape(1, d), b.reshape(1, d)]
    if gates is not None:
        in_specs = [pl.BlockSpec((tm, LANES), lambda i, e, j: (i, 0))] + in_specs
        args = [gates] + args
    return pl.pallas_call(
        functools.partial(_ffn_kernel, gated=gates is not None),
        grid=(n // tm, n_exp, f // tf),
        in_specs=in_specs,
        out_specs=[row, row],
        out_shape=[jax.ShapeDtypeStruct((n, d), F32), jax.ShapeDtypeStruct((n, d), BF16)],
        scratch_shapes=[pltpu.VMEM((tm, d), F32)],
        compiler_params=_params("parallel", "arbitrary", "arbitrary"),
        name="moe_ffn" if gates is not None else "dense_ffn",
    )(*args)


def _router_kernel(h_ref, w_ref, o_ref):
    logits = jnp.dot(h_ref[...], w_ref[...], preferred_element_type=F32, precision=lax.Precision.HIGHEST)
    lane = lax.broadcasted_iota(jnp.int32, logits.shape, 1)
    logits = jnp.where(lane < N_EXPERTS, logits, NEG_BIG)
    v1 = jnp.max(logits, axis=-1, keepdims=True)
    i1 = jnp.min(jnp.where(logits == v1, lane, LANES), axis=-1, keepdims=True)
    rest = jnp.where(lane == i1, NEG_BIG, logits)
    v2 = jnp.max(rest, axis=-1, keepdims=True)
    i2 = jnp.min(jnp.where(rest == v2, lane, LANES), axis=-1, keepdims=True)
    e2 = jnp.exp(v2 - v1)
    w1 = 1.0 / (1.0 + e2)
    w2 = e2 / (1.0 + e2)
    o_ref[...] = jnp.where(lane == i1, w1, 0.0) + jnp.where(lane == i2, w2, 0.0)


def _router_call(h32, w_router):
    n, d = h32.shape
    tm = _pick_tile(n, (1280, 640, 512, 256, 128))
    w = jnp.pad(w_router, ((0, 0), (0, LANES - N_EXPERTS)))
    return pl.pallas_call(
        _router_kernel,
        grid=(n // tm,),
        in_specs=[pl.BlockSpec((tm, d), lambda i: (i, 0)), pl.BlockSpec((d, LANES), lambda i: (0, 0))],
        out_specs=pl.BlockSpec((tm, LANES), lambda i: (i, 0)),
        out_shape=jax.ShapeDtypeStruct((n, LANES), F32),
        compiler_params=_params("parallel"),
        name="router",
    )(h32, w)


def _permute_w_in(w):
    sizes = (D_MODEL, D_MODEL, ATTN_WIDTH, KV_WIDTH, KV_WIDTH, D_INNER, D_INNER, BC_WIDTH, BC_WIDTH, SSD_HEADS)
    offs = [0]
    for s in sizes:
        offs.append(offs[-1] + s)
    g_a, g_s, q, k, v, z, xs, bm, cm, dt = (w[:, offs[i]:offs[i + 1]] for i in range(len(sizes)))
    q = q * (HEAD_DIM ** -0.5)
    tail = jnp.zeros((w.shape[0], PROJ_WIDTH - OFF_DT - SSD_HEADS), w.dtype)
    return jnp.concatenate([z, xs, g_a, g_s, q, bm, cm, k, v, dt, tail], axis=1).astype(BF16)


def kernel(x, meta_tokens, ln_in_g, ln_in_b, w_in, conv_w, conv_b, dt_bias, a_log, d_skip, ssd_norm_g, sinks,
           w_attn_out, w_ssd_out, w_o, ln1_g, ln1_b, ffn_wg, ffn_wu, ffn_wd, moe_router, moe_wg, moe_wu, moe_wd,
           ln2_g, ln2_b):
    batch, seq, d = x.shape
    assert d == D_MODEL and seq % BLOCK == 0
    t_len = seq + BLOCK
    nb = t_len // BLOCK
    n = batch * t_len

    h0 = jnp.concatenate([jnp.zeros((batch, PAD, d), x.dtype),
                          jnp.broadcast_to(meta_tokens.astype(x.dtype)[None], (batch, N_META, d)), x], axis=1)
    h32, hb = _ln_call(h0.reshape(n, d), ln_in_g, ln_in_b)

    pos = (jnp.arange(t_len) - PAD).astype(F32)
    inv_freq = ROPE_THETA ** (-jnp.arange(0, HEAD_DIM, 2, dtype=F32) / HEAD_DIM)
    ang = pos[:, None] * inv_freq[None, :]
    cos, sin = jnp.cos(ang), jnp.sin(ang)
    cos_t = jnp.tile(jnp.concatenate([cos, cos], axis=1), (1, LANES // HEAD_DIM))
    sin_t = jnp.tile(jnp.concatenate([-sin, sin], axis=1), (1, LANES // HEAD_DIM))

    head_of_lane = jnp.arange(D_INNER) // SSD_HEAD_DIM
    expand = (jnp.arange(LANES)[:, None] == head_of_lane[None, :]).astype(BF16)

    for l in range(DEPTH):
        proj = _proj_call(hb, _permute_w_in(w_in[l]))
        attn = _attn_call(proj, sinks[l], cos_t, sin_t, batch, nb)
        yn = _ssd_call(proj, conv_w[l], conv_b[l], dt_bias[l], a_log[l], d_skip[l], ssd_norm_g[l], expand, batch, nb)
        h32, hb = _merge_call(attn, yn, proj, h32, w_attn_out[l].astype(BF16), w_ssd_out[l].astype(BF16),
                              w_o[l].astype(BF16), ln1_g[l], ln1_b[l])
        if l % 2 == 0:
            i = l // 2
            h32, hb = _ffn_call(hb, h32, ffn_wg[i][None].astype(BF16), ffn_wu[i][None].astype(BF16),
                                ffn_wd[i][None].astype(BF16), ln2_g[l], ln2_b[l])
        else:
            i = l // 2
            gates = _router_call(h32, moe_router[i])
            h32, hb = _ffn_call(hb, h32, moe_wg[i].astype(BF16), moe_wu[i].astype(BF16), moe_wd[i].astype(BF16),
                                ln2_g[l], ln2_b[l], gates=gates)
    return h32.reshape(batch, t_len, d)[:, BLOCK:]
```

```python
import jax
import jax.numpy as jnp
from jax import lax
from jax.experimental import pallas as pl
from jax.experimental.pallas import tpu as pltpu

D_MODEL = 1024
DEPTH = 4
N_META = 16
BLOCK = 128
PAD = BLOCK - N_META
HEAD_DIM = 64
N_Q_HEADS = 16
N_KV_HEADS = 4
ROPE_THETA = 10000.0
ATTN_WIDTH = N_Q_HEADS * HEAD_DIM
KV_WIDTH = N_KV_HEADS * HEAD_DIM
D_INNER = 2 * D_MODEL
SSD_HEAD_DIM = 64
SSD_HEADS = D_INNER // SSD_HEAD_DIM
SSD_GROUPS = 4
SSD_STATE = 128
GROUP_WIDTH = D_INNER // SSD_GROUPS
CONV_WIDTH = 4
BC_WIDTH = SSD_GROUPS * SSD_STATE
D_FF = 2816
N_EXPERTS = 8
D_FF_EXPERT = 3584
ALPHA = (2 * DEPTH) ** 0.25
LN_EPS = 1e-5
RMS_EPS = 1e-5

LANES = 128
NEG_BIG = -1e30
VMEM_LIMIT = 56 * 1024 * 1024

OFF_Z = 0
OFF_XS = OFF_Z + D_INNER
OFF_GA = OFF_XS + D_INNER
OFF_GS = OFF_GA + D_MODEL
OFF_Q = OFF_GS + D_MODEL
OFF_B = OFF_Q + ATTN_WIDTH
OFF_C = OFF_B + BC_WIDTH
OFF_K = OFF_C + BC_WIDTH
OFF_V = OFF_K + KV_WIDTH
OFF_DT = OFF_V + KV_WIDTH
PROJ_TILE = 1024
PROJ_WIDTH = ((OFF_DT + LANES + PROJ_TILE - 1) // PROJ_TILE) * PROJ_TILE

F32 = jnp.float32
BF16 = jnp.bfloat16


def _pick_tile(n, candidates):
    for c in candidates:
        if n % c == 0:
            return c
    raise ValueError(f"no tile in {candidates} divides {n}")


def _params(*semantics):
    return pltpu.CompilerParams(dimension_semantics=semantics, vmem_limit_bytes=VMEM_LIMIT)


def _layer_norm(x, g, b):
    mu = jnp.mean(x, axis=-1, keepdims=True)
    xc = x - mu
    var = jnp.mean(xc * xc, axis=-1, keepdims=True)
    return xc * lax.rsqrt(var + LN_EPS) * g + b


def _silu(x):
    return x * jax.nn.sigmoid(x)


def _dot(a, b):
    return jnp.dot(a, b, preferred_element_type=F32)


def _dot_nt(a, b):
    return lax.dot_general(a, b, (((1,), (1,)), ((), ())), preferred_element_type=F32)


def _split2(x):
    hi = x.astype(BF16)
    lo = (x - hi.astype(F32)).astype(BF16)
    return hi, lo


def _dot_exact_rhs(x, m):
    hi, lo = _split2(x)
    return _dot(hi, m) + _dot(lo, m)


def _ln_kernel(x_ref, g_ref, b_ref, o32_ref, o16_ref):
    y = _layer_norm(x_ref[...], g_ref[...], b_ref[...])
    o32_ref[...] = y
    o16_ref[...] = y.astype(BF16)


def _ln_call(x, g, b):
    n, d = x.shape
    tm = _pick_tile(n, (1280, 640, 512, 256, 128))
    row = pl.BlockSpec((tm, d), lambda i: (i, 0))
    vec = pl.BlockSpec((1, d), lambda i: (0, 0))
    return pl.pallas_call(
        _ln_kernel,
        grid=(n // tm,),
        in_specs=[row, vec, vec],
        out_specs=[row, row],
        out_shape=[jax.ShapeDtypeStruct((n, d), F32), jax.ShapeDtypeStruct((n, d), BF16)],
        compiler_params=_params("parallel"),
        name="ln_in",
    )(x, g.reshape(1, d), b.reshape(1, d))


def _proj_kernel(x_ref, w_ref, o_ref):
    o_ref[...] = _dot(x_ref[...], w_ref[...])


def _proj_call(hb, w):
    n, d = hb.shape
    width = w.shape[1]
    tm = _pick_tile(n, (1280, 640, 512, 256, 128))
    tn = PROJ_TILE
    return pl.pallas_call(
        _proj_kernel,
        grid=(n // tm, width // tn),
        in_specs=[pl.BlockSpec((tm, d), lambda i, j: (i, 0)), pl.BlockSpec((d, tn), lambda i, j: (0, j))],
        out_specs=pl.BlockSpec((tm, tn), lambda i, j: (i, j)),
        out_shape=jax.ShapeDtypeStruct((n, width), F32),
        compiler_params=_params("parallel", "arbitrary"),
        name="in_proj",
    )(hb, w)


def _rope(x, cos, sin_signed):
    width = x.shape[1]
    reps = width // LANES
    c = jnp.tile(cos, (1, reps))
    s = jnp.tile(sin_signed, (1, reps))
    lane = lax.broadcasted_iota(jnp.int32, x.shape, 1)
    first_half = (lane % HEAD_DIM) < (HEAD_DIM // 2)
    partner = jnp.where(first_half, pltpu.roll(x, width - HEAD_DIM // 2, 1), pltpu.roll(x, HEAD_DIM // 2, 1))
    return x * c + partner * s


def _half_split(t):
    lane = lax.broadcasted_iota(jnp.int32, t.shape, 1)
    low = lane < HEAD_DIM
    swapped = pltpu.roll(t, HEAD_DIM, 1)
    zero = jnp.zeros_like(t)
    a_lo = jnp.where(low, t, zero).astype(BF16)
    a_hi = jnp.where(low, zero, swapped).astype(BF16)
    b_lo = jnp.where(low, swapped, zero).astype(BF16)
    b_hi = jnp.where(low, zero, t).astype(BF16)
    return (a_lo, a_hi), (b_lo, b_hi)


def _attn_kernel(sink_ref, q_ref, kc_ref, kp_ref, vc_ref, vp_ref, cc_ref, sc_ref, cp_ref, sp_ref, o_ref):
    n = pl.program_id(1)
    cos_c, sin_c = cc_ref[...], sc_ref[...]
    q = _rope(q_ref[...], cos_c, sin_c)
    k = jnp.concatenate([_rope(kp_ref[...], cp_ref[...], sp_ref[...]), _rope(kc_ref[...], cos_c, sin_c)], axis=0)
    v = jnp.concatenate([vp_ref[...], vc_ref[...]], axis=0)

    i = lax.broadcasted_iota(jnp.int32, (BLOCK, 2 * BLOCK), 0)
    jj = lax.broadcasted_iota(jnp.int32, (BLOCK, 2 * BLOCK), 1)
    visible = (jj > i) & (jj <= i + BLOCK) & (jj + (n - 1) * BLOCK >= PAD)
    bias = jnp.where(visible, 0.0, NEG_BIG).astype(F32)

    k_ops, v_ops = [], []
    for t in range(KV_WIDTH // LANES):
        ka, kb = _half_split(k[:, t * LANES:(t + 1) * LANES])
        va, vb = _half_split(v[:, t * LANES:(t + 1) * LANES])
        k_ops += [ka, kb]
        v_ops += [va, vb]

    heads_per_kv = N_Q_HEADS // N_KV_HEADS
    for p in range(ATTN_WIDTH // LANES):
        g = (2 * p) // heads_per_kv
        qp = q[:, p * LANES:(p + 1) * LANES].astype(BF16)
        out = jnp.zeros((BLOCK, LANES), F32)
        for half in range(2):
            sink = sink_ref[2 * p + half]
            s = _dot_nt(qp, k_ops[g][half]) + bias
            m = jnp.maximum(jnp.max(s, axis=-1, keepdims=True), sink)
            e = jnp.exp(s - m)
            denom = jnp.sum(e, axis=-1, keepdims=True) + jnp.exp(sink - m)
            out = out + _dot(e.astype(BF16), v_ops[g][half]) / denom
        o_ref[:, p * LANES:(p + 1) * LANES] = out.astype(o_ref.dtype)


def _attn_call(proj, sinks, cos_t, sin_t, batch, nb):
    n = proj.shape[0]

    def row(b, i):
        return b * nb + i

    def prev(b, i):
        return b * nb + jnp.maximum(i - 1, 0)

    kv_blk = OFF_K // KV_WIDTH
    in_specs = [
        pl.BlockSpec(memory_space=pltpu.SMEM),
        pl.BlockSpec((BLOCK, ATTN_WIDTH), lambda b, i: (row(b, i), OFF_Q // ATTN_WIDTH)),
        pl.BlockSpec((BLOCK, KV_WIDTH), lambda b, i: (row(b, i), kv_blk)),
        pl.BlockSpec((BLOCK, KV_WIDTH), lambda b, i: (prev(b, i), kv_blk)),
        pl.BlockSpec((BLOCK, KV_WIDTH), lambda b, i: (row(b, i), kv_blk + 1)),
        pl.BlockSpec((BLOCK, KV_WIDTH), lambda b, i: (prev(b, i), kv_blk + 1)),
        pl.BlockSpec((BLOCK, LANES), lambda b, i: (i, 0)),
        pl.BlockSpec((BLOCK, LANES), lambda b, i: (i, 0)),
        pl.BlockSpec((BLOCK, LANES), lambda b, i: (jnp.maximum(i - 1, 0), 0)),
        pl.BlockSpec((BLOCK, LANES), lambda b, i: (jnp.maximum(i - 1, 0), 0)),
    ]
    return pl.pallas_call(
        _attn_kernel,
        grid=(batch, nb),
        in_specs=in_specs,
        out_specs=pl.BlockSpec((BLOCK, ATTN_WIDTH), lambda b, i: (row(b, i), 0)),
        out_shape=jax.ShapeDtypeStruct((n, ATTN_WIDTH), BF16),
        compiler_params=_params("parallel", "arbitrary"),
        name="swa_attn",
    )(sinks, proj, proj, proj, proj, proj, cos_t, sin_t, cos_t, sin_t)


CARRY = 8


def _causal_conv(buf_ref, raw, w_ref, b_ref):
    buf_ref[CARRY:CARRY + BLOCK, :] = raw
    acc = b_ref[...] + w_ref[CONV_WIDTH - 1:CONV_WIDTH, :] * raw
    for j in range(CONV_WIDTH - 1):
        start = CARRY - (CONV_WIDTH - 1) + j
        acc = acc + w_ref[j:j + 1, :] * buf_ref[start:start + BLOCK, :]
    buf_ref[0:CARRY, :] = raw[BLOCK - CARRY:, :]
    return acc


def _ssd_kernel(xs_ref, z_ref, b_ref, c_ref, dt_ref, cwx_ref, cwb_ref, cwc_ref, cbx_ref, cbb_ref, cbc_ref,
                dtb_ref, alog_ref, dskip_ref, ng_ref, exp_ref, o_ref,
                bufx, bufb, bufc, state, y_scr):
    c = pl.program_id(1)

    @pl.when(c == 0)
    def _():
        state[...] = jnp.zeros_like(state)
        bufx[0:CARRY, :] = jnp.zeros((CARRY, D_INNER), F32)
        bufb[0:CARRY, :] = jnp.zeros((CARRY, BC_WIDTH), F32)
        bufc[0:CARRY, :] = jnp.zeros((CARRY, BC_WIDTH), F32)

    row = lax.broadcasted_iota(jnp.int32, (BLOCK, 1), 0)
    valid = ((row + c * BLOCK) >= PAD).astype(F32)

    xs = _silu(_causal_conv(bufx, xs_ref[...] * valid, cwx_ref, cbx_ref)) * valid
    bm = _silu(_causal_conv(bufb, b_ref[...] * valid, cwb_ref, cbb_ref)).astype(BF16)
    cm = _silu(_causal_conv(bufc, c_ref[...] * valid, cwc_ref, cbc_ref)).astype(BF16)

    dt = jnp.logaddexp(dt_ref[...] + dtb_ref[...], 0.0)
    da = dt * (-jnp.exp(alog_ref[...]))
    li = lax.broadcasted_iota(jnp.int32, (BLOCK, BLOCK), 0)
    si = lax.broadcasted_iota(jnp.int32, (BLOCK, BLOCK), 1)
    causal = li >= si
    tril = jnp.where(causal, 1.0, 0.0).astype(BF16)
    hi = da.astype(BF16)
    r1 = da - hi.astype(F32)
    mid = r1.astype(BF16)
    lo = (r1 - mid.astype(F32)).astype(BF16)
    a_cum = _dot(tril, hi) + _dot(tril, mid) + _dot(tril, lo)
    a_cum_t = a_cum.T
    a_last = a_cum[BLOCK - 1:BLOCK, :]
    exp_a = jnp.exp(a_cum)
    decay_to_end = jnp.exp(a_last - a_cum)

    expand = exp_ref[...]
    dt_x = _dot_exact_rhs(dt, expand)
    exp_a_x = _dot_exact_rhs(exp_a, expand)
    decay_x = _dot_exact_rhs(decay_to_end, expand)
    chunk_decay_x = exp_a_x[BLOCK - 1:BLOCK, :]

    xdt = xs * dt_x
    xdec_b = (xdt * decay_x).astype(BF16)

    lane = lax.broadcasted_iota(jnp.int32, (BLOCK, LANES), 1)
    low = lane < SSD_HEAD_DIM
    heads_per_group = SSD_HEADS // SSD_GROUPS
    for g in range(SSD_GROUPS):
        gs = slice(g * GROUP_WIDTH, (g + 1) * GROUP_WIDTH)
        bg = bm[:, g * SSD_STATE:(g + 1) * SSD_STATE]
        cg = cm[:, g * SSD_STATE:(g + 1) * SSD_STATE]
        cb = _dot_nt(cg, bg)
        bg_t = bg.astype(F32).T.astype(BF16)
        new_states = _dot(bg_t, xdec_b[:, gs])
        st = state[:, gs]
        y_off = _dot(cg, st.astype(BF16)) * exp_a_x[:, gs]
        state[:, gs] = st * chunk_decay_x[:, gs] + new_states
        for j in range(heads_per_group // 2):
            h0 = g * heads_per_group + 2 * j
            ms = []
            for h in (h0, h0 + 1):
                seg = a_cum[:, h:h + 1] - a_cum_t[h:h + 1, :]
                ms.append((cb * jnp.exp(jnp.where(causal, seg, NEG_BIG))).astype(BF16))
            lhs = jnp.concatenate(ms, axis=1)
            xp = xdt[:, h0 * SSD_HEAD_DIM:h0 * SSD_HEAD_DIM + LANES]
            zero = jnp.zeros_like(xp)
            rhs = jnp.concatenate([jnp.where(low, xp, zero), jnp.where(low, zero, xp)], axis=0).astype(BF16)
            y_scr[:, h0 * SSD_HEAD_DIM:h0 * SSD_HEAD_DIM + LANES] = (
                _dot(lhs, rhs) + y_off[:, 2 * j * SSD_HEAD_DIM:2 * j * SSD_HEAD_DIM + LANES])

    y = y_scr[...] + xs * dskip_ref[...]
    u = y * _silu(z_ref[...])
    for g in range(SSD_GROUPS):
        gs = slice(g * GROUP_WIDTH, (g + 1) * GROUP_WIDTH)
        ug = u[:, gs]
        scale = lax.rsqrt(jnp.mean(ug * ug, axis=-1, keepdims=True) + RMS_EPS)
        o_ref[:, gs] = (ug * scale * ng_ref[:, gs]).astype(o_ref.dtype)


def _ssd_call(proj, conv_w, conv_b, dt_bias, a_log, d_skip, norm_g, expand, batch, nc):
    n = proj.shape[0]

    def row(b, i):
        return b * nc + i

    def full(shape):
        return pl.BlockSpec(shape, lambda b, i: (0,) * len(shape))

    cwx, cwb, cwc = conv_w[:, :D_INNER], conv_w[:, D_INNER:D_INNER + BC_WIDTH], conv_w[:, D_INNER + BC_WIDTH:]
    cbx = conv_b[None, :D_INNER]
    cbb = conv_b[None, D_INNER:D_INNER + BC_WIDTH]
    cbc = conv_b[None, D_INNER + BC_WIDTH:]
    pad = LANES - SSD_HEADS
    dtb = jnp.pad(dt_bias, (0, pad))[None, :]
    alog = jnp.pad(a_log, (0, pad))[None, :]
    dskip = jnp.repeat(d_skip, SSD_HEAD_DIM)[None, :]
    in_specs = [
        pl.BlockSpec((BLOCK, D_INNER), lambda b, i: (row(b, i), OFF_XS // D_INNER)),
        pl.BlockSpec((BLOCK, D_INNER), lambda b, i: (row(b, i), OFF_Z // D_INNER)),
        pl.BlockSpec((BLOCK, BC_WIDTH), lambda b, i: (row(b, i), OFF_B // BC_WIDTH)),
        pl.BlockSpec((BLOCK, BC_WIDTH), lambda b, i: (row(b, i), OFF_C // BC_WIDTH)),
        pl.BlockSpec((BLOCK, LANES), lambda b, i: (row(b, i), OFF_DT // LANES)),
        full((CONV_WIDTH, D_INNER)), full((CONV_WIDTH, BC_WIDTH)), full((CONV_WIDTH, BC_WIDTH)),
        full((1, D_INNER)), full((1, BC_WIDTH)), full((1, BC_WIDTH)),
        full((1, LANES)), full((1, LANES)), full((1, D_INNER)), full((1, D_INNER)),
        full((LANES, D_INNER)),
    ]
    scratch = [
        pltpu.VMEM((CARRY + BLOCK, D_INNER), F32),
        pltpu.VMEM((CARRY + BLOCK, BC_WIDTH), F32),
        pltpu.VMEM((CARRY + BLOCK, BC_WIDTH), F32),
        pltpu.VMEM((SSD_STATE, D_INNER), F32),
        pltpu.VMEM((BLOCK, D_INNER), F32),
    ]
    return pl.pallas_call(
        _ssd_kernel,
        grid=(batch, nc),
        in_specs=in_specs,
        out_specs=pl.BlockSpec((BLOCK, D_INNER), lambda b, i: (row(b, i), 0)),
        out_shape=jax.ShapeDtypeStruct((n, D_INNER), BF16),
        scratch_shapes=scratch,
        compiler_params=_params("arbitrary", "arbitrary"),
        name="ssd",
    )(proj, proj, proj, proj, proj, cwx, cwb, cwc, cbx, cbb, cbc, dtb, alog, dskip, norm_g[None, :], expand)


def _merge_kernel(attn_ref, y_ref, ga_ref, gs_ref, h_ref, wa_ref, ws_ref, wo_ref, g_ref, b_ref, o32_ref, o16_ref):
    a = _dot(attn_ref[...], wa_ref[...])
    s = _dot(y_ref[...], ws_ref[...])
    merged = jax.nn.sigmoid(ga_ref[...]) * a + jax.nn.sigmoid(gs_ref[...]) * s
    mix = _dot(merged.astype(BF16), wo_ref[...])
    y = _layer_norm(ALPHA * h_ref[...] + mix, g_ref[...], b_ref[...])
    o32_ref[...] = y
    o16_ref[...] = y.astype(BF16)


def _merge_call(attn, yn, proj, h32, wa, ws, wo, g, b):
    n, d = h32.shape
    tm = _pick_tile(n, (640, 512, 256, 128))
    row = pl.BlockSpec((tm, d), lambda i: (i, 0))
    vec = pl.BlockSpec((1, d), lambda i: (0, 0))

    def const(shape):
        return pl.BlockSpec(shape, lambda i: (0, 0), pipeline_mode=pl.Buffered(1))

    in_specs = [
        pl.BlockSpec((tm, ATTN_WIDTH), lambda i: (i, 0)),
        pl.BlockSpec((tm, D_INNER), lambda i: (i, 0)),
        pl.BlockSpec((tm, D_MODEL), lambda i: (i, OFF_GA // D_MODEL)),
        pl.BlockSpec((tm, D_MODEL), lambda i: (i, OFF_GS // D_MODEL)),
        row,
        const((ATTN_WIDTH, d)), const((D_INNER, d)), const((d, d)),
        vec, vec,
    ]
    return pl.pallas_call(
        _merge_kernel,
        grid=(n // tm,),
        in_specs=in_specs,
        out_specs=[row, row],
        out_shape=[jax.ShapeDtypeStruct((n, d), F32), jax.ShapeDtypeStruct((n, d), BF16)],
        compiler_params=_params("parallel"),
        name="merge_out",
    )(attn, yn, proj, proj, h32, wa, ws, wo, g.reshape(1, d), b.reshape(1, d))


def _ffn_kernel(hb_ref, h_ref, wg_ref, wu_ref, wd_ref, g_ref, b_ref, o32_ref, o16_ref, acc):
    j = pl.program_id(1)

    @pl.when(j == 0)
    def _():
        acc[...] = jnp.zeros_like(acc)

    hb = hb_ref[...]
    act = (_silu(_dot(hb, wg_ref[...])) * _dot(hb, wu_ref[...])).astype(BF16)
    acc[...] += _dot(act, wd_ref[...])

    @pl.when(j == pl.num_programs(1) - 1)
    def _():
        y = _layer_norm(ALPHA * h_ref[...] + acc[...], g_ref[...], b_ref[...])
        o32_ref[...] = y
        o16_ref[...] = y.astype(BF16)


def _ffn_call(hb, h32, wg, wu, wd, g, b):
    n, d = h32.shape
    f = wg.shape[1]
    tm = _pick_tile(n, (640, 512, 256, 128))
    tf = _pick_tile(f, (1408, 896, 512, 256, 128))
    row = pl.BlockSpec((tm, d), lambda i, j: (i, 0))
    vec = pl.BlockSpec((1, d), lambda i, j: (0, 0))
    in_specs = [
        row, row,
        pl.BlockSpec((d, tf), lambda i, j: (0, j)),
        pl.BlockSpec((d, tf), lambda i, j: (0, j)),
        pl.BlockSpec((tf, d), lambda i, j: (j, 0)),
        vec, vec,
    ]
    return pl.pallas_call(
        _ffn_kernel,
        grid=(n // tm, f // tf),
        in_specs=in_specs,
        out_specs=[row, row],
        out_shape=[jax.ShapeDtypeStruct((n, d), F32), jax.ShapeDtypeStruct((n, d), BF16)],
        scratch_shapes=[pltpu.VMEM((tm, d), F32)],
        compiler_params=_params("parallel", "arbitrary"),
        name="dense_ffn",
    )(hb, h32, wg, wu, wd, g.reshape(1, d), b.reshape(1, d))


R_I1, R_I2, R_W1, R_W2, R_RANK1, R_RANK2 = range(6)


def _router_kernel(h_ref, w_ref, route_ref, count_ref, carry):
    @pl.when(pl.program_id(0) == 0)
    def _():
        carry[...] = jnp.zeros_like(carry)

    logits = jnp.dot(h_ref[...], w_ref[...], preferred_element_type=F32, precision=lax.Precision.HIGHEST)
    tm = logits.shape[0]
    lane = lax.broadcasted_iota(jnp.int32, logits.shape, 1)
    logits = jnp.where(lane < N_EXPERTS, logits, NEG_BIG)
    v1 = jnp.max(logits, axis=-1, keepdims=True)
    i1 = jnp.min(jnp.where(logits == v1, lane, LANES), axis=-1, keepdims=True)
    rest = jnp.where(lane == i1, NEG_BIG, logits)
    v2 = jnp.max(rest, axis=-1, keepdims=True)
    i2 = jnp.min(jnp.where(rest == v2, lane, LANES), axis=-1, keepdims=True)
    e2 = jnp.exp(v2 - v1)
    w1 = 1.0 / (1.0 + e2)
    w2 = e2 / (1.0 + e2)

    hot1 = jnp.where(lane == i1, 1.0, 0.0)
    hot2 = jnp.where(lane == i2, 1.0, 0.0)
    ri = lax.broadcasted_iota(jnp.int32, (tm, tm), 0)
    ci = lax.broadcasted_iota(jnp.int32, (tm, tm), 1)
    before = jnp.where(ri > ci, 1.0, 0.0).astype(BF16)
    base = carry[0:1, :]
    count1 = jnp.sum(hot1, axis=0, keepdims=True)
    rank1 = jnp.sum(hot1 * (_dot(before, hot1.astype(BF16)) + base), axis=-1, keepdims=True)
    rank2 = jnp.sum(hot2 * (_dot(before, hot2.astype(BF16)) + base + count1), axis=-1, keepdims=True)
    total = base + count1 + jnp.sum(hot2, axis=0, keepdims=True)
    carry[0:1, :] = total

    rec = jnp.zeros(logits.shape, F32)
    for slot, val in ((R_I1, i1.astype(F32)), (R_I2, i2.astype(F32)), (R_W1, w1), (R_W2, w2),
                      (R_RANK1, rank1), (R_RANK2, rank2)):
        rec = jnp.where(lane == slot, val, rec)
    route_ref[...] = rec
    count_ref[...] = jnp.broadcast_to(total, count_ref.shape)


def _router_call(h32, w_router):
    n, d = h32.shape
    tm = _pick_tile(n, (1280, 640, 512, 256, 128))
    w = jnp.pad(w_router, ((0, 0), (0, LANES - N_EXPERTS)))
    return pl.pallas_call(
        _router_kernel,
        grid=(n // tm,),
        in_specs=[pl.BlockSpec((tm, d), lambda i: (i, 0)), pl.BlockSpec((d, LANES), lambda i: (0, 0))],
        out_specs=[pl.BlockSpec((tm, LANES), lambda i: (i, 0)), pl.BlockSpec((8, LANES), lambda i: (0, 0))],
        out_shape=[jax.ShapeDtypeStruct((n, LANES), F32), jax.ShapeDtypeStruct((8, LANES), F32)],
        scratch_shapes=[pltpu.VMEM((8, LANES), F32)],
        compiler_params=_params("arbitrary"),
        name="router",
    )(h32, w)


MOE_TILE = 512
MOE_FF_CHUNK = 512


def _row_copy(src_hbm, dst, src_row, dst_row, sem):
    return pltpu.make_async_copy(src_hbm.at[pl.ds(src_row, 1), :], dst.at[pl.ds(dst_row, 1), :], sem)


def _dispatch_kernel(pos_ref, h_hbm, init_hbm, xs_hbm, sem):
    del init_hbm
    tm = pos_ref.shape[-1] // 2
    base = pl.program_id(0) * tm

    def start(r, _):
        _row_copy(h_hbm, xs_hbm, base + r, pos_ref[0, r], sem).start()
        _row_copy(h_hbm, xs_hbm, base + r, pos_ref[0, tm + r], sem).start()
        return _

    def wait(r, _):
        _row_copy(h_hbm, xs_hbm, base + r, pos_ref[0, r], sem).wait()
        _row_copy(h_hbm, xs_hbm, base + r, pos_ref[0, tm + r], sem).wait()
        return _

    lax.fori_loop(0, tm, start, 0)
    lax.fori_loop(0, tm, wait, 0)


def _dispatch_call(pos, h32, n_rows):
    n, d = h32.shape
    nt, _, two_tm = pos.shape
    return pl.pallas_call(
        _dispatch_kernel,
        grid=(nt,),
        in_specs=[pl.BlockSpec((None, 1, two_tm), lambda i: (i, 0, 0), memory_space=pltpu.SMEM),
                  pl.BlockSpec(memory_space=pl.ANY), pl.BlockSpec(memory_space=pl.ANY)],
        out_specs=pl.BlockSpec(memory_space=pl.ANY),
        out_shape=jax.ShapeDtypeStruct((n_rows, d), F32),
        scratch_shapes=[pltpu.SemaphoreType.DMA(())],
        input_output_aliases={2: 0},
        compiler_params=_params("arbitrary"),
        name="moe_dispatch",
    )(pos, h32, jnp.zeros((n_rows, d), F32))


def _grouped_ffn_kernel(te_ref, na_ref, x_ref, wg_ref, wu_ref, wd_ref, o_ref):
    del te_ref
    i = pl.program_id(0)

    @pl.when(i < na_ref[0])
    def _():
        xb = x_ref[...].astype(BF16)
        f = wg_ref.shape[1]
        acc = jnp.zeros(o_ref.shape, F32)
        for c in range(f // MOE_FF_CHUNK):
            cs = slice(c * MOE_FF_CHUNK, (c + 1) * MOE_FF_CHUNK)
            act = (_silu(_dot(xb, wg_ref[:, cs])) * _dot(xb, wu_ref[:, cs])).astype(BF16)
            acc = acc + _dot(act, wd_ref[cs, :])
        o_ref[...] = acc

    @pl.when(i >= na_ref[0])
    def _():
        o_ref[...] = jnp.zeros_like(o_ref)


def _grouped_ffn_call(tile_expert, n_active, xs, wg, wu, wd):
    n_rows, d = xs.shape
    f = wg.shape[2]
    tm = MOE_TILE
    grid_spec = pltpu.PrefetchScalarGridSpec(
        num_scalar_prefetch=2,
        grid=(n_rows // tm,),
        in_specs=[
            pl.BlockSpec((tm, d), lambda i, te, na: (i, 0)),
            pl.BlockSpec((None, d, f), lambda i, te, na: (te[i], 0, 0), pipeline_mode=pl.Buffered(1)),
            pl.BlockSpec((None, d, f), lambda i, te, na: (te[i], 0, 0), pipeline_mode=pl.Buffered(1)),
            pl.BlockSpec((None, f, d), lambda i, te, na: (te[i], 0, 0), pipeline_mode=pl.Buffered(1)),
        ],
        out_specs=pl.BlockSpec((tm, d), lambda i, te, na: (i, 0)),
    )
    return pl.pallas_call(
        _grouped_ffn_kernel,
        grid_spec=grid_spec,
        out_shape=jax.ShapeDtypeStruct((n_rows, d), F32),
        compiler_params=_params("arbitrary"),
        name="moe_grouped_ffn",
    )(tile_expert, n_active, xs, wg, wu, wd)


def _combine_kernel(pos_ref, ys_hbm, route_ref, h_ref, g_ref, b_ref, o32_ref, o16_ref, y1, y2, sem):
    tm = y1.shape[0]

    def start(r, _):
        _row_copy(ys_hbm, y1, pos_ref[0, r], r, sem).start()
        _row_copy(ys_hbm, y2, pos_ref[0, tm + r], r, sem).start()
        return _

    def wait(r, _):
        _row_copy(ys_hbm, y1, pos_ref[0, r], r, sem).wait()
        _row_copy(ys_hbm, y2, pos_ref[0, tm + r], r, sem).wait()
        return _

    lax.fori_loop(0, tm, start, 0)
    lax.fori_loop(0, tm, wait, 0)
    route = route_ref[...]
    w1 = route[:, R_W1:R_W1 + 1]
    w2 = route[:, R_W2:R_W2 + 1]
    f = w1 * y1[...] + w2 * y2[...]
    y = _layer_norm(ALPHA * h_ref[...] + f, g_ref[...], b_ref[...])
    o32_ref[...] = y
    o16_ref[...] = y.astype(BF16)


def _combine_call(pos, ys, route, h32, g, b):
    n, d = h32.shape
    nt, _, two_tm = pos.shape
    tm = two_tm // 2
    row = pl.BlockSpec((tm, d), lambda i: (i, 0))
    vec = pl.BlockSpec((1, d), lambda i: (0, 0))
    return pl.pallas_call(
        _combine_kernel,
        grid=(nt,),
        in_specs=[pl.BlockSpec((None, 1, two_tm), lambda i: (i, 0, 0), memory_space=pltpu.SMEM),
                  pl.BlockSpec(memory_space=pl.ANY),
                  pl.BlockSpec((tm, LANES), lambda i: (i, 0)), row, vec, vec],
        out_specs=[row, row],
        out_shape=[jax.ShapeDtypeStruct((n, d), F32), jax.ShapeDtypeStruct((n, d), BF16)],
        scratch_shapes=[pltpu.VMEM((tm, d), F32), pltpu.VMEM((tm, d), F32), pltpu.SemaphoreType.DMA(())],
        compiler_params=_params("arbitrary"),
        name="moe_combine",
    )(pos, ys, route, h32, g.reshape(1, d), b.reshape(1, d))


def _tiled_positions(pos1, pos2, tm):
    nt = pos1.shape[0] // tm
    return jnp.concatenate([pos1.reshape(nt, 1, tm), pos2.reshape(nt, 1, tm)], axis=2)


def _moe_layer(h32, w_router, wg, wu, wd, g, b):
    n, d = h32.shape
    tm = MOE_TILE
    n_rows = ((2 * n + N_EXPERTS * (tm - 1) + tm - 1) // tm) * tm
    n_tiles = n_rows // tm

    route, counts = _router_call(h32, w_router)
    count = counts[0, :N_EXPERTS].astype(jnp.int32)
    padded = ((count + tm - 1) // tm) * tm
    end = jnp.cumsum(padded)
    start = end - padded
    i1 = route[:, R_I1].astype(jnp.int32)
    i2 = route[:, R_I2].astype(jnp.int32)
    pos1 = start[i1] + route[:, R_RANK1].astype(jnp.int32)
    pos2 = start[i2] + route[:, R_RANK2].astype(jnp.int32)
    n_active = (end[-1] // tm).astype(jnp.int32)
    tile_start = jnp.arange(n_tiles, dtype=jnp.int32) * tm
    tile_expert = jnp.minimum(jnp.sum(tile_start[:, None] >= end[None, :], axis=1), N_EXPERTS - 1).astype(jnp.int32)
    tile_expert = jnp.where(tile_start < end[-1], tile_expert, tile_expert[jnp.maximum(n_active - 1, 0)])

    td = _pick_tile(n, (1280, 640, 512, 256, 128))
    xs = _dispatch_call(_tiled_positions(pos1, pos2, td), h32, n_rows)
    ys = _grouped_ffn_call(tile_expert, n_active.reshape(1), xs, wg, wu, wd)
    tc = _pick_tile(n, (256, 128))
    return _combine_call(_tiled_positions(pos1, pos2, tc), ys, route, h32, g, b)


def _permute_w_in(w):
    sizes = (D_MODEL, D_MODEL, ATTN_WIDTH, KV_WIDTH, KV_WIDTH, D_INNER, D_INNER, BC_WIDTH, BC_WIDTH, SSD_HEADS)
    offs = [0]
    for s in sizes:
        offs.append(offs[-1] + s)
    g_a, g_s, q, k, v, z, xs, bm, cm, dt = (w[:, offs[i]:offs[i + 1]] for i in range(len(sizes)))
    q = q * (HEAD_DIM ** -0.5)
    tail = jnp.zeros((w.shape[0], PROJ_WIDTH - OFF_DT - SSD_HEADS), w.dtype)
    return jnp.concatenate([z, xs, g_a, g_s, q, bm, cm, k, v, dt, tail], axis=1).astype(BF16)


def kernel(x, meta_tokens, ln_in_g, ln_in_b, w_in, conv_w, conv_b, dt_bias, a_log, d_skip, ssd_norm_g, sinks,
           w_attn_out, w_ssd_out, w_o, ln1_g, ln1_b, ffn_wg, ffn_wu, ffn_wd, moe_router, moe_wg, moe_wu, moe_wd,
           ln2_g, ln2_b):
    batch, seq, d = x.shape
    assert d == D_MODEL and seq % BLOCK == 0
    t_len = seq + BLOCK
    nb = t_len // BLOCK
    n = batch * t_len

    h0 = jnp.concatenate([jnp.zeros((batch, PAD, d), x.dtype),
                          jnp.broadcast_to(meta_tokens.astype(x.dtype)[None], (batch, N_META, d)), x], axis=1)
    h32, hb = _ln_call(h0.reshape(n, d), ln_in_g, ln_in_b)

    pos = (jnp.arange(t_len) - PAD).astype(F32)
    inv_freq = ROPE_THETA ** (-jnp.arange(0, HEAD_DIM, 2, dtype=F32) / HEAD_DIM)
    ang = pos[:, None] * inv_freq[None, :]
    cos, sin = jnp.cos(ang), jnp.sin(ang)
    cos_t = jnp.tile(jnp.concatenate([cos, cos], axis=1), (1, LANES // HEAD_DIM))
    sin_t = jnp.tile(jnp.concatenate([-sin, sin], axis=1), (1, LANES // HEAD_DIM))

    head_of_lane = jnp.arange(D_INNER) // SSD_HEAD_DIM
    expand = (jnp.arange(LANES)[:, None] == head_of_lane[None, :]).astype(BF16)

    for l in range(DEPTH):
        proj = _proj_call(hb, _permute_w_in(w_in[l]))
        attn = _attn_call(proj, sinks[l], cos_t, sin_t, batch, nb)
        yn = _ssd_call(proj, conv_w[l], conv_b[l], dt_bias[l], a_log[l], d_skip[l], ssd_norm_g[l], expand, batch, nb)
        h32, hb = _merge_call(attn, yn, proj, h32, w_attn_out[l].astype(BF16), w_ssd_out[l].astype(BF16),
                              w_o[l].astype(BF16), ln1_g[l], ln1_b[l])
        i = l // 2
        if l % 2 == 0:
            h32, hb = _ffn_call(hb, h32, ffn_wg[i].astype(BF16), ffn_wu[i].astype(BF16), ffn_wd[i].astype(BF16),
                                ln2_g[l], ln2_b[l])
        else:
            h32, hb = _moe_layer(h32, moe_router[i], moe_wg[i].astype(BF16), moe_wu[i].astype(BF16),
                                 moe_wd[i].astype(BF16), ln2_g[l], ln2_b[l])
    return h32.reshape(batch, t_len, d)[:, BLOCK:]
```

```python
import jax
import jax.numpy as jnp
from jax import lax
from jax.experimental import pallas as pl
from jax.experimental.pallas import tpu as pltpu

D_MODEL = 1024
DEPTH = 4
N_META = 16
BLOCK = 128
PAD = BLOCK - N_META
HEAD_DIM = 64
N_Q_HEADS = 16
N_KV_HEADS = 4
ROPE_THETA = 10000.0
ATTN_WIDTH = N_Q_HEADS * HEAD_DIM
KV_WIDTH = N_KV_HEADS * HEAD_DIM
D_INNER = 2 * D_MODEL
SSD_HEAD_DIM = 64
SSD_HEADS = D_INNER // SSD_HEAD_DIM
SSD_GROUPS = 4
SSD_STATE = 128
GROUP_WIDTH = D_INNER // SSD_GROUPS
CONV_WIDTH = 4
BC_WIDTH = SSD_GROUPS * SSD_STATE
D_FF = 2816
N_EXPERTS = 8
D_FF_EXPERT = 3584
ALPHA = (2 * DEPTH) ** 0.25
LN_EPS = 1e-5
RMS_EPS = 1e-5

LANES = 128
NEG_BIG = -1e30
VMEM_LIMIT = 56 * 1024 * 1024

OFF_Z = 0
OFF_XS = OFF_Z + D_INNER
OFF_GA = OFF_XS + D_INNER
OFF_GS = OFF_GA + D_MODEL
OFF_Q = OFF_GS + D_MODEL
OFF_B = OFF_Q + ATTN_WIDTH
OFF_C = OFF_B + BC_WIDTH
OFF_K = OFF_C + BC_WIDTH
OFF_V = OFF_K + KV_WIDTH
OFF_DT = OFF_V + KV_WIDTH
PROJ_TILE = 1024
PROJ_WIDTH = ((OFF_DT + LANES + PROJ_TILE - 1) // PROJ_TILE) * PROJ_TILE

F32 = jnp.float32
BF16 = jnp.bfloat16


def _pick_tile(n, candidates):
    for c in candidates:
        if n % c == 0:
            return c
    raise ValueError(f"no tile in {candidates} divides {n}")


def _params(*semantics):
    return pltpu.CompilerParams(dimension_semantics=semantics, vmem_limit_bytes=VMEM_LIMIT)


def _layer_norm(x, g, b):
    mu = jnp.mean(x, axis=-1, keepdims=True)
    xc = x - mu
    var = jnp.mean(xc * xc, axis=-1, keepdims=True)
    return xc * lax.rsqrt(var + LN_EPS) * g + b


def _silu(x):
    return x * jax.nn.sigmoid(x)


def _dot(a, b):
    return jnp.dot(a, b, preferred_element_type=F32)


def _dot_nt(a, b):
    return lax.dot_general(a, b, (((1,), (1,)), ((), ())), preferred_element_type=F32)


def _dot_tn(a, b):
    return lax.dot_general(a, b, (((0,), (0,)), ((), ())), preferred_element_type=F32)


def _split2(x):
    hi = x.astype(BF16)
    lo = (x - hi.astype(F32)).astype(BF16)
    return hi, lo


def _dot_exact_rhs(x, m):
    hi, lo = _split2(x)
    return _dot(hi, m) + _dot(lo, m)


def _ln_kernel(x_ref, g_ref, b_ref, o32_ref, o16_ref):
    y = _layer_norm(x_ref[...], g_ref[...], b_ref[...])
    o32_ref[...] = y
    o16_ref[...] = y.astype(BF16)


def _ln_call(x, g, b):
    n, d = x.shape
    tm = _pick_tile(n, (1280, 640, 512, 256, 128))
    row = pl.BlockSpec((tm, d), lambda i: (i, 0))
    vec = pl.BlockSpec((1, d), lambda i: (0, 0))
    return pl.pallas_call(
        _ln_kernel,
        grid=(n // tm,),
        in_specs=[row, vec, vec],
        out_specs=[row, row],
        out_shape=[jax.ShapeDtypeStruct((n, d), F32), jax.ShapeDtypeStruct((n, d), BF16)],
        compiler_params=_params("parallel"),
        name="ln_in",
    )(x, g.reshape(1, d), b.reshape(1, d))


def _proj_kernel(x_ref, w_ref, o_ref):
    o_ref[...] = _dot(x_ref[...], w_ref[...])


def _proj_call(hb, w):
    n, d = hb.shape
    width = w.shape[1]
    tm = _pick_tile(n, (1280, 640, 512, 256, 128))
    tn = PROJ_TILE
    return pl.pallas_call(
        _proj_kernel,
        grid=(n // tm, width // tn),
        in_specs=[pl.BlockSpec((tm, d), lambda i, j: (i, 0)), pl.BlockSpec((d, tn), lambda i, j: (0, j))],
        out_specs=pl.BlockSpec((tm, tn), lambda i, j: (i, j)),
        out_shape=jax.ShapeDtypeStruct((n, width), F32),
        compiler_params=_params("parallel", "arbitrary"),
        name="in_proj",
    )(hb, w)


def _half_split(t):
    lane = lax.broadcasted_iota(jnp.int32, t.shape, 1)
    low = lane < HEAD_DIM
    swapped = pltpu.roll(t, HEAD_DIM, 1)
    zero = jnp.zeros_like(t)
    a_lo = jnp.where(low, t, zero).astype(BF16)
    a_hi = jnp.where(low, zero, swapped).astype(BF16)
    b_lo = jnp.where(low, swapped, zero).astype(BF16)
    b_hi = jnp.where(low, zero, t).astype(BF16)
    return (a_lo, a_hi), (b_lo, b_hi)


def _attn_kernel(sink_ref, q_ref, k_ref, v_ref, cos_ref, sin_ref, perm_ref, o_ref, kop, vop, s_scr, e_scr):
    n = pl.program_id(1)

    @pl.when(n == 0)
    def _():
        kop[...] = jnp.zeros_like(kop)
        vop[...] = jnp.zeros_like(vop)

    cos, sin = cos_ref[...], sin_ref[...]
    perm = perm_ref[...]

    def rope(x):
        return x * cos + _dot(x.astype(BF16), perm) * sin

    for t in range(KV_WIDTH // LANES):
        ks = _half_split(rope(k_ref[:, t * LANES:(t + 1) * LANES]))
        vs = _half_split(v_ref[:, t * LANES:(t + 1) * LANES])
        for which in range(2):
            g = 2 * t + which
            for half in range(2):
                kop[g, half, 0:BLOCK, :] = kop[g, half, BLOCK:2 * BLOCK, :]
                vop[g, half, 0:BLOCK, :] = vop[g, half, BLOCK:2 * BLOCK, :]
                kop[g, half, BLOCK:2 * BLOCK, :] = ks[which][half]
                vop[g, half, BLOCK:2 * BLOCK, :] = vs[which][half]

    tiles_per_kv = (N_Q_HEADS // N_KV_HEADS) * HEAD_DIM // LANES
    jj = lax.broadcasted_iota(jnp.int32, (2 * BLOCK, BLOCK), 0)
    i = lax.broadcasted_iota(jnp.int32, (2 * BLOCK, BLOCK), 1)
    visible = (jj > i) & (jj <= i + BLOCK) & (jj + (n - 1) * BLOCK >= PAD)
    bias = jnp.tile(jnp.where(visible, 0.0, NEG_BIG).astype(F32), (1, tiles_per_kv))

    for g in range(N_KV_HEADS):
        t0 = g * tiles_per_kv
        qg = jnp.concatenate(
            [rope(q_ref[:, (t0 + t) * LANES:(t0 + t + 1) * LANES]).astype(BF16) for t in range(tiles_per_kv)], axis=0)
        for half in range(2):
            s_scr[2 * g + half] = _dot_nt(kop[g, half], qg)
    inv_denom = []
    for g in range(N_KV_HEADS):
        t0 = g * tiles_per_kv
        for half in range(2):
            s = s_scr[2 * g + half] + bias
            sink = jnp.concatenate(
                [jnp.full((1, BLOCK), sink_ref[2 * (t0 + t) + half], F32) for t in range(tiles_per_kv)], axis=1)
            m = jnp.maximum(jnp.max(s, axis=0, keepdims=True), sink)
            e = jnp.exp(s - m)
            inv_denom.append(1.0 / (jnp.sum(e, axis=0, keepdims=True) + jnp.exp(sink - m)))
            e_scr[2 * g + half] = e.astype(BF16)
    for g in range(N_KV_HEADS):
        t0 = g * tiles_per_kv
        acc = (_dot_tn(vop[g, 0], e_scr[2 * g]) * inv_denom[2 * g]
               + _dot_tn(vop[g, 1], e_scr[2 * g + 1]) * inv_denom[2 * g + 1])
        for t in range(tiles_per_kv):
            o_ref[(t0 + t) * LANES:(t0 + t + 1) * LANES, :] = acc[:, t * BLOCK:(t + 1) * BLOCK].astype(o_ref.dtype)


def _attn_call(proj, sinks, cos_t, sin_t, perm, batch, nb):
    n = proj.shape[0]

    def row(b, i):
        return b * nb + i

    kv_blk = OFF_K // KV_WIDTH
    in_specs = [
        pl.BlockSpec(memory_space=pltpu.SMEM),
        pl.BlockSpec((BLOCK, ATTN_WIDTH), lambda b, i: (row(b, i), OFF_Q // ATTN_WIDTH)),
        pl.BlockSpec((BLOCK, KV_WIDTH), lambda b, i: (row(b, i), kv_blk)),
        pl.BlockSpec((BLOCK, KV_WIDTH), lambda b, i: (row(b, i), kv_blk + 1)),
        pl.BlockSpec((BLOCK, LANES), lambda b, i: (i, 0)),
        pl.BlockSpec((BLOCK, LANES), lambda b, i: (i, 0)),
        pl.BlockSpec((LANES, LANES), lambda b, i: (0, 0)),
    ]
    operand = pltpu.VMEM((N_KV_HEADS, 2, 2 * BLOCK, LANES), BF16)
    return pl.pallas_call(
        _attn_kernel,
        grid=(batch, nb),
        in_specs=in_specs,
        out_specs=pl.BlockSpec((ATTN_WIDTH, BLOCK), lambda b, i: (0, row(b, i))),
        out_shape=jax.ShapeDtypeStruct((ATTN_WIDTH, n), BF16),
        scratch_shapes=[operand, operand,
                        pltpu.VMEM((2 * N_KV_HEADS, 2 * BLOCK, ATTN_WIDTH // N_KV_HEADS), F32),
                        pltpu.VMEM((2 * N_KV_HEADS, 2 * BLOCK, ATTN_WIDTH // N_KV_HEADS), BF16)],
        compiler_params=_params("parallel", "arbitrary"),
        name="swa_attn",
    )(sinks, proj, proj, proj, cos_t, sin_t, perm)


CARRY = 8


def _causal_conv(buf_ref, raw, w_ref, b_ref):
    buf_ref[CARRY:CARRY + BLOCK, :] = raw
    acc = b_ref[...] + w_ref[CONV_WIDTH - 1:CONV_WIDTH, :] * raw
    for j in range(CONV_WIDTH - 1):
        start = CARRY - (CONV_WIDTH - 1) + j
        acc = acc + w_ref[j:j + 1, :] * buf_ref[start:start + BLOCK, :]
    buf_ref[0:CARRY, :] = raw[BLOCK - CARRY:, :]
    return acc


def _ssd_kernel(xs_ref, z_ref, b_ref, c_ref, dt_ref, cwx_ref, cwb_ref, cwc_ref, cbx_ref, cbb_ref, cbc_ref,
                dtb_ref, alog_ref, dskip_ref, ng_ref, exp_ref, o_ref,
                bufx, bufb, bufc, state, y_scr):
    c = pl.program_id(1)

    @pl.when(c == 0)
    def _():
        state[...] = jnp.zeros_like(state)
        bufx[0:CARRY, :] = jnp.zeros((CARRY, D_INNER), F32)
        bufb[0:CARRY, :] = jnp.zeros((CARRY, BC_WIDTH), F32)
        bufc[0:CARRY, :] = jnp.zeros((CARRY, BC_WIDTH), F32)

    row = lax.broadcasted_iota(jnp.int32, (BLOCK, 1), 0)
    valid = ((row + c * BLOCK) >= PAD).astype(F32)

    xs = _silu(_causal_conv(bufx, xs_ref[...] * valid, cwx_ref, cbx_ref)) * valid
    bm = _silu(_causal_conv(bufb, b_ref[...] * valid, cwb_ref, cbb_ref)).astype(BF16)
    cm = _silu(_causal_conv(bufc, c_ref[...] * valid, cwc_ref, cbc_ref)).astype(BF16)

    dt = jnp.logaddexp(dt_ref[...] + dtb_ref[...], 0.0)
    da = dt * (-jnp.exp(alog_ref[...]))
    li = lax.broadcasted_iota(jnp.int32, (BLOCK, BLOCK), 0)
    si = lax.broadcasted_iota(jnp.int32, (BLOCK, BLOCK), 1)
    causal = li >= si
    tril = jnp.where(causal, 1.0, 0.0).astype(BF16)
    hi = da.astype(BF16)
    r1 = da - hi.astype(F32)
    mid = r1.astype(BF16)
    lo = (r1 - mid.astype(F32)).astype(BF16)
    a_cum = _dot(tril, hi) + _dot(tril, mid) + _dot(tril, lo)
    a_cum_t = a_cum.T
    a_last = a_cum[BLOCK - 1:BLOCK, :]
    exp_a = jnp.exp(a_cum)
    decay_to_end = jnp.exp(a_last - a_cum)

    expand = exp_ref[...]
    dt_x = _dot_exact_rhs(dt, expand)
    exp_a_x = _dot_exact_rhs(exp_a, expand)
    decay_x = _dot_exact_rhs(decay_to_end, expand)
    chunk_decay_x = exp_a_x[BLOCK - 1:BLOCK, :]

    xdt = xs * dt_x
    xdec_b = (xdt * decay_x).astype(BF16)

    lane = lax.broadcasted_iota(jnp.int32, (BLOCK, LANES), 1)
    low = lane < SSD_HEAD_DIM
    heads_per_group = SSD_HEADS // SSD_GROUPS
    for g in range(SSD_GROUPS):
        gs = slice(g * GROUP_WIDTH, (g + 1) * GROUP_WIDTH)
        bg = bm[:, g * SSD_STATE:(g + 1) * SSD_STATE]
        cg = cm[:, g * SSD_STATE:(g + 1) * SSD_STATE]
        cb = _dot_nt(cg, bg)
        bg_t = bg.astype(F32).T.astype(BF16)
        new_states = _dot(bg_t, xdec_b[:, gs])
        st = state[:, gs]
        y_off = _dot(cg, st.astype(BF16)) * exp_a_x[:, gs]
        state[:, gs] = st * chunk_decay_x[:, gs] + new_states
        for j in range(heads_per_group // 2):
            h0 = g * heads_per_group + 2 * j
            ms = []
            for h in (h0, h0 + 1):
                seg = a_cum[:, h:h + 1] - a_cum_t[h:h + 1, :]
                ms.append((cb * jnp.exp(jnp.where(causal, seg, NEG_BIG))).astype(BF16))
            lhs = jnp.concatenate(ms, axis=1)
            xp = xdt[:, h0 * SSD_HEAD_DIM:h0 * SSD_HEAD_DIM + LANES]
            zero = jnp.zeros_like(xp)
            rhs = jnp.concatenate([jnp.where(low, xp, zero), jnp.where(low, zero, xp)], axis=0).astype(BF16)
            y_scr[:, h0 * SSD_HEAD_DIM:h0 * SSD_HEAD_DIM + LANES] = (
                _dot(lhs, rhs) + y_off[:, 2 * j * SSD_HEAD_DIM:2 * j * SSD_HEAD_DIM + LANES])

    y = y_scr[...] + xs * dskip_ref[...]
    u = y * _silu(z_ref[...])
    for g in range(SSD_GROUPS):
        gs = slice(g * GROUP_WIDTH, (g + 1) * GROUP_WIDTH)
        ug = u[:, gs]
        scale = lax.rsqrt(jnp.mean(ug * ug, axis=-1, keepdims=True) + RMS_EPS)
        o_ref[:, gs] = (ug * scale * ng_ref[:, gs]).astype(o_ref.dtype)


def _ssd_call(proj, conv_w, conv_b, dt_bias, a_log, d_skip, norm_g, expand, batch, nc):
    n = proj.shape[0]

    def row(b, i):
        return b * nc + i

    def full(shape):
        return pl.BlockSpec(shape, lambda b, i: (0,) * len(shape))

    cwx, cwb, cwc = conv_w[:, :D_INNER], conv_w[:, D_INNER:D_INNER + BC_WIDTH], conv_w[:, D_INNER + BC_WIDTH:]
    cbx = conv_b[None, :D_INNER]
    cbb = conv_b[None, D_INNER:D_INNER + BC_WIDTH]
    cbc = conv_b[None, D_INNER + BC_WIDTH:]
    pad = LANES - SSD_HEADS
    dtb = jnp.pad(dt_bias, (0, pad))[None, :]
    alog = jnp.pad(a_log, (0, pad))[None, :]
    dskip = jnp.repeat(d_skip, SSD_HEAD_DIM)[None, :]
    in_specs = [
        pl.BlockSpec((BLOCK, D_INNER), lambda b, i: (row(b, i), OFF_XS // D_INNER)),
        pl.BlockSpec((BLOCK, D_INNER), lambda b, i: (row(b, i), OFF_Z // D_INNER)),
        pl.BlockSpec((BLOCK, BC_WIDTH), lambda b, i: (row(b, i), OFF_B // BC_WIDTH)),
        pl.BlockSpec((BLOCK, BC_WIDTH), lambda b, i: (row(b, i), OFF_C // BC_WIDTH)),
        pl.BlockSpec((BLOCK, LANES), lambda b, i: (row(b, i), OFF_DT // LANES)),
        full((CONV_WIDTH, D_INNER)), full((CONV_WIDTH, BC_WIDTH)), full((CONV_WIDTH, BC_WIDTH)),
        full((1, D_INNER)), full((1, BC_WIDTH)), full((1, BC_WIDTH)),
        full((1, LANES)), full((1, LANES)), full((1, D_INNER)), full((1, D_INNER)),
        full((LANES, D_INNER)),
    ]
    scratch = [
        pltpu.VMEM((CARRY + BLOCK, D_INNER), F32),
        pltpu.VMEM((CARRY + BLOCK, BC_WIDTH), F32),
        pltpu.VMEM((CARRY + BLOCK, BC_WIDTH), F32),
        pltpu.VMEM((SSD_STATE, D_INNER), F32),
        pltpu.VMEM((BLOCK, D_INNER), F32),
    ]
    return pl.pallas_call(
        _ssd_kernel,
        grid=(batch, nc),
        in_specs=in_specs,
        out_specs=pl.BlockSpec((BLOCK, D_INNER), lambda b, i: (row(b, i), 0)),
        out_shape=jax.ShapeDtypeStruct((n, D_INNER), BF16),
        scratch_shapes=scratch,
        compiler_params=_params("arbitrary", "arbitrary"),
        name="ssd",
    )(proj, proj, proj, proj, proj, cwx, cwb, cwc, cbx, cbb, cbc, dtb, alog, dskip, norm_g[None, :], expand)


def _merge_kernel(attn_ref, y_ref, ga_ref, gs_ref, h_ref, wa_ref, ws_ref, wo_ref, g_ref, b_ref, o32_ref, o16_ref):
    a = _dot_tn(attn_ref[...], wa_ref[...])
    s = _dot(y_ref[...], ws_ref[...])
    merged = jax.nn.sigmoid(ga_ref[...]) * a + jax.nn.sigmoid(gs_ref[...]) * s
    mix = _dot(merged.astype(BF16), wo_ref[...])
    y = _layer_norm(ALPHA * h_ref[...] + mix, g_ref[...], b_ref[...])
    o32_ref[...] = y
    o16_ref[...] = y.astype(BF16)


def _merge_call(attn, yn, proj, h32, wa, ws, wo, g, b):
    n, d = h32.shape
    tm = _pick_tile(n, (640, 512, 256, 128))
    row = pl.BlockSpec((tm, d), lambda i: (i, 0))
    vec = pl.BlockSpec((1, d), lambda i: (0, 0))

    def const(shape):
        return pl.BlockSpec(shape, lambda i: (0, 0), pipeline_mode=pl.Buffered(1))

    in_specs = [
        pl.BlockSpec((ATTN_WIDTH, tm), lambda i: (0, i)),
        pl.BlockSpec((tm, D_INNER), lambda i: (i, 0)),
        pl.BlockSpec((tm, D_MODEL), lambda i: (i, OFF_GA // D_MODEL)),
        pl.BlockSpec((tm, D_MODEL), lambda i: (i, OFF_GS // D_MODEL)),
        row,
        const((ATTN_WIDTH, d)), const((D_INNER, d)), const((d, d)),
        vec, vec,
    ]
    return pl.pallas_call(
        _merge_kernel,
        grid=(n // tm,),
        in_specs=in_specs,
        out_specs=[row, row],
        out_shape=[jax.ShapeDtypeStruct((n, d), F32), jax.ShapeDtypeStruct((n, d), BF16)],
        compiler_params=_params("parallel"),
        name="merge_out",
    )(attn, yn, proj, proj, h32, wa, ws, wo, g.reshape(1, d), b.reshape(1, d))


FF_CHUNK = 256


def _swiglu(xb, wg_ref, wu_ref, wd_ref):
    f = wg_ref.shape[1]
    acc = jnp.zeros((xb.shape[0], wd_ref.shape[1]), F32)
    for c in range(f // FF_CHUNK):
        cs = slice(c * FF_CHUNK, (c + 1) * FF_CHUNK)
        act = (_silu(_dot(xb, wg_ref[:, cs])) * _dot(xb, wu_ref[:, cs])).astype(BF16)
        acc = acc + _dot(act, wd_ref[cs, :])
    return acc


def _ffn_kernel(hb_ref, h_ref, wg_ref, wu_ref, wd_ref, g_ref, b_ref, o32_ref, o16_ref):
    f = _swiglu(hb_ref[...], wg_ref, wu_ref, wd_ref)
    y = _layer_norm(ALPHA * h_ref[...] + f, g_ref[...], b_ref[...])
    o32_ref[...] = y
    o16_ref[...] = y.astype(BF16)


def _ffn_call(hb, h32, wg, wu, wd, g, b):
    n, d = h32.shape
    f = wg.shape[1]
    assert f % FF_CHUNK == 0
    tm = _pick_tile(n, (640, 512, 256, 128))
    row = pl.BlockSpec((tm, d), lambda i: (i, 0))
    vec = pl.BlockSpec((1, d), lambda i: (0, 0))

    def const(shape):
        return pl.BlockSpec(shape, lambda i: (0, 0), pipeline_mode=pl.Buffered(1))

    return pl.pallas_call(
        _ffn_kernel,
        grid=(n // tm,),
        in_specs=[row, row, const((d, f)), const((d, f)), const((f, d)), vec, vec],
        out_specs=[row, row],
        out_shape=[jax.ShapeDtypeStruct((n, d), F32), jax.ShapeDtypeStruct((n, d), BF16)],
        compiler_params=_params("parallel"),
        name="dense_ffn",
    )(hb, h32, wg, wu, wd, g.reshape(1, d), b.reshape(1, d))


R_I1, R_I2, R_W1, R_W2, R_RANK1, R_RANK2 = range(6)


def _router_kernel(h_ref, w_ref, route_ref, count_ref, carry):
    @pl.when(pl.program_id(0) == 0)
    def _():
        carry[...] = jnp.zeros_like(carry)

    logits = jnp.dot(h_ref[...], w_ref[...], preferred_element_type=F32, precision=lax.Precision.HIGHEST)
    tm = logits.shape[0]
    lane = lax.broadcasted_iota(jnp.int32, logits.shape, 1)
    logits = jnp.where(lane < N_EXPERTS, logits, NEG_BIG)
    v1 = jnp.max(logits, axis=-1, keepdims=True)
    i1 = jnp.min(jnp.where(logits == v1, lane, LANES), axis=-1, keepdims=True)
    rest = jnp.where(lane == i1, NEG_BIG, logits)
    v2 = jnp.max(rest, axis=-1, keepdims=True)
    i2 = jnp.min(jnp.where(rest == v2, lane, LANES), axis=-1, keepdims=True)
    e2 = jnp.exp(v2 - v1)
    w1 = 1.0 / (1.0 + e2)
    w2 = e2 / (1.0 + e2)

    hot1 = jnp.where(lane == i1, 1.0, 0.0)
    hot2 = jnp.where(lane == i2, 1.0, 0.0)
    ri = lax.broadcasted_iota(jnp.int32, (tm, tm), 0)
    ci = lax.broadcasted_iota(jnp.int32, (tm, tm), 1)
    before = jnp.where(ri > ci, 1.0, 0.0).astype(BF16)
    base = carry[0:1, :]
    count1 = jnp.sum(hot1, axis=0, keepdims=True)
    rank1 = jnp.sum(hot1 * (_dot(before, hot1.astype(BF16)) + base), axis=-1, keepdims=True)
    rank2 = jnp.sum(hot2 * (_dot(before, hot2.astype(BF16)) + base + count1), axis=-1, keepdims=True)
    total = base + count1 + jnp.sum(hot2, axis=0, keepdims=True)
    carry[0:1, :] = total

    rec = jnp.zeros(logits.shape, F32)
    for slot, val in ((R_I1, i1.astype(F32)), (R_I2, i2.astype(F32)), (R_W1, w1), (R_W2, w2),
                      (R_RANK1, rank1), (R_RANK2, rank2)):
        rec = jnp.where(lane == slot, val, rec)
    route_ref[...] = rec
    count_ref[...] = jnp.broadcast_to(total, count_ref.shape)


def _router_call(h32, w_router):
    n, d = h32.shape
    tm = _pick_tile(n, (1280, 640, 512, 256, 128))
    w = jnp.pad(w_router, ((0, 0), (0, LANES - N_EXPERTS)))
    return pl.pallas_call(
        _router_kernel,
        grid=(n // tm,),
        in_specs=[pl.BlockSpec((tm, d), lambda i: (i, 0)), pl.BlockSpec((d, LANES), lambda i: (0, 0))],
        out_specs=[pl.BlockSpec((tm, LANES), lambda i: (i, 0)), pl.BlockSpec((8, LANES), lambda i: (0, 0))],
        out_shape=[jax.ShapeDtypeStruct((n, LANES), F32), jax.ShapeDtypeStruct((8, LANES), F32)],
        scratch_shapes=[pltpu.VMEM((8, LANES), F32)],
        compiler_params=_params("arbitrary"),
        name="router",
    )(h32, w)


MOE_TILE = 512


def _row_copy(src, dst, src_row, dst_row, sem):
    return pltpu.make_async_copy(src.at[pl.ds(src_row, 1), :], dst.at[pl.ds(dst_row, 1), :], sem)


def _dispatch_kernel(pos_ref, h_ref, init_hbm, xs_hbm, sem):
    del init_hbm
    tm = h_ref.shape[0]

    def start(r, _):
        _row_copy(h_ref, xs_hbm, r, pos_ref[0, r], sem).start()
        _row_copy(h_ref, xs_hbm, r, pos_ref[0, tm + r], sem).start()
        return _

    def wait(r, _):
        _row_copy(h_ref, xs_hbm, r, pos_ref[0, r], sem).wait()
        _row_copy(h_ref, xs_hbm, r, pos_ref[0, tm + r], sem).wait()
        return _

    lax.fori_loop(0, tm, start, 0)
    lax.fori_loop(0, tm, wait, 0)


def _dispatch_call(pos, h32, n_rows):
    n, d = h32.shape
    nt, _, two_tm = pos.shape
    return pl.pallas_call(
        _dispatch_kernel,
        grid=(nt,),
        in_specs=[pl.BlockSpec((None, 1, two_tm), lambda i: (i, 0, 0), memory_space=pltpu.SMEM),
                  pl.BlockSpec((two_tm // 2, d), lambda i: (i, 0)), pl.BlockSpec(memory_space=pl.ANY)],
        out_specs=pl.BlockSpec(memory_space=pl.ANY),
        out_shape=jax.ShapeDtypeStruct((n_rows, d), F32),
        scratch_shapes=[pltpu.SemaphoreType.DMA(())],
        input_output_aliases={2: 0},
        compiler_params=_params("arbitrary"),
        name="moe_dispatch",
    )(pos, h32, jnp.zeros((n_rows, d), F32))


def _grouped_ffn_kernel(te_ref, na_ref, x_ref, wg_ref, wu_ref, wd_ref, o_ref):
    del te_ref
    i = pl.program_id(0)

    @pl.when(i < na_ref[0])
    def _():
        o_ref[...] = _swiglu(x_ref[...].astype(BF16), wg_ref, wu_ref, wd_ref)

    @pl.when(i >= na_ref[0])
    def _():
        o_ref[...] = jnp.zeros_like(o_ref)


def _grouped_ffn_call(tile_expert, n_active, xs, wg, wu, wd):
    n_rows, d = xs.shape
    f = wg.shape[2]
    tm = MOE_TILE
    grid_spec = pltpu.PrefetchScalarGridSpec(
        num_scalar_prefetch=2,
        grid=(n_rows // tm,),
        in_specs=[
            pl.BlockSpec((tm, d), lambda i, te, na: (i, 0)),
            pl.BlockSpec((None, d, f), lambda i, te, na: (te[i], 0, 0), pipeline_mode=pl.Buffered(1)),
            pl.BlockSpec((None, d, f), lambda i, te, na: (te[i], 0, 0), pipeline_mode=pl.Buffered(1)),
            pl.BlockSpec((None, f, d), lambda i, te, na: (te[i], 0, 0), pipeline_mode=pl.Buffered(1)),
        ],
        out_specs=pl.BlockSpec((tm, d), lambda i, te, na: (i, 0)),
    )
    return pl.pallas_call(
        _grouped_ffn_kernel,
        grid_spec=grid_spec,
        out_shape=jax.ShapeDtypeStruct((n_rows, d), F32),
        compiler_params=_params("arbitrary"),
        name="moe_grouped_ffn",
    )(tile_expert, n_active, xs, wg, wu, wd)


def _combine_kernel(pos_ref, ys_hbm, route_ref, h_ref, g_ref, b_ref, o32_ref, o16_ref, y1, y2, sem):
    tm = y1.shape[0]

    def start(r, _):
        _row_copy(ys_hbm, y1, pos_ref[0, r], r, sem).start()
        _row_copy(ys_hbm, y2, pos_ref[0, tm + r], r, sem).start()
        return _

    def wait(r, _):
        _row_copy(ys_hbm, y1, pos_ref[0, r], r, sem).wait()
        _row_copy(ys_hbm, y2, pos_ref[0, tm + r], r, sem).wait()
        return _

    lax.fori_loop(0, tm, start, 0)
    lax.fori_loop(0, tm, wait, 0)
    route = route_ref[...]
    w1 = route[:, R_W1:R_W1 + 1]
    w2 = route[:, R_W2:R_W2 + 1]
    f = w1 * y1[...] + w2 * y2[...]
    y = _layer_norm(ALPHA * h_ref[...] + f, g_ref[...], b_ref[...])
    o32_ref[...] = y
    o16_ref[...] = y.astype(BF16)


def _combine_call(pos, ys, route, h32, g, b):
    n, d = h32.shape
    nt, _, two_tm = pos.shape
    tm = two_tm // 2
    row = pl.BlockSpec((tm, d), lambda i: (i, 0))
    vec = pl.BlockSpec((1, d), lambda i: (0, 0))
    return pl.pallas_call(
        _combine_kernel,
        grid=(nt,),
        in_specs=[pl.BlockSpec((None, 1, two_tm), lambda i: (i, 0, 0), memory_space=pltpu.SMEM),
                  pl.BlockSpec(memory_space=pl.ANY),
                  pl.BlockSpec((tm, LANES), lambda i: (i, 0)), row, vec, vec],
        out_specs=[row, row],
        out_shape=[jax.ShapeDtypeStruct((n, d), F32), jax.ShapeDtypeStruct((n, d), BF16)],
        scratch_shapes=[pltpu.VMEM((tm, d), F32), pltpu.VMEM((tm, d), F32), pltpu.SemaphoreType.DMA(())],
        compiler_params=_params("arbitrary"),
        name="moe_combine",
    )(pos, ys, route, h32, g.reshape(1, d), b.reshape(1, d))


def _tiled_positions(pos1, pos2, tm):
    nt = pos1.shape[0] // tm
    return jnp.concatenate([pos1.reshape(nt, 1, tm), pos2.reshape(nt, 1, tm)], axis=2)


def _moe_layer(h32, w_router, wg, wu, wd, g, b):
    n, d = h32.shape
    tm = MOE_TILE
    n_rows = ((2 * n + N_EXPERTS * (tm - 1) + tm - 1) // tm) * tm
    n_tiles = n_rows // tm

    route, counts = _router_call(h32, w_router)
    count = counts[0, :N_EXPERTS].astype(jnp.int32)
    padded = ((count + tm - 1) // tm) * tm
    end = jnp.cumsum(padded)
    start = end - padded
    i1 = route[:, R_I1].astype(jnp.int32)
    i2 = route[:, R_I2].astype(jnp.int32)
    pos1 = start[i1] + route[:, R_RANK1].astype(jnp.int32)
    pos2 = start[i2] + route[:, R_RANK2].astype(jnp.int32)
    n_active = (end[-1] // tm).astype(jnp.int32)
    tile_start = jnp.arange(n_tiles, dtype=jnp.int32) * tm
    tile_expert = jnp.minimum(jnp.sum(tile_start[:, None] >= end[None, :], axis=1), N_EXPERTS - 1).astype(jnp.int32)
    tile_expert = jnp.where(tile_start < end[-1], tile_expert, tile_expert[jnp.maximum(n_active - 1, 0)])

    td = _pick_tile(n, (1280, 640, 512, 256, 128))
    xs = _dispatch_call(_tiled_positions(pos1, pos2, td), h32, n_rows)
    ys = _grouped_ffn_call(tile_expert, n_active.reshape(1), xs, wg, wu, wd)
    tc = _pick_tile(n, (256, 128))
    return _combine_call(_tiled_positions(pos1, pos2, tc), ys, route, h32, g, b)


def _permute_w_in(w):
    sizes = (D_MODEL, D_MODEL, ATTN_WIDTH, KV_WIDTH, KV_WIDTH, D_INNER, D_INNER, BC_WIDTH, BC_WIDTH, SSD_HEADS)
    offs = [0]
    for s in sizes:
        offs.append(offs[-1] + s)
    g_a, g_s, q, k, v, z, xs, bm, cm, dt = (w[:, offs[i]:offs[i + 1]] for i in range(len(sizes)))
    q = q * (HEAD_DIM ** -0.5)
    tail = jnp.zeros((w.shape[0], PROJ_WIDTH - OFF_DT - SSD_HEADS), w.dtype)
    return jnp.concatenate([z, xs, g_a, g_s, q, bm, cm, k, v, dt, tail], axis=1).astype(BF16)


def kernel(x, meta_tokens, ln_in_g, ln_in_b, w_in, conv_w, conv_b, dt_bias, a_log, d_skip, ssd_norm_g, sinks,
           w_attn_out, w_ssd_out, w_o, ln1_g, ln1_b, ffn_wg, ffn_wu, ffn_wd, moe_router, moe_wg, moe_wu, moe_wd,
           ln2_g, ln2_b):
    batch, seq, d = x.shape
    assert d == D_MODEL and seq % BLOCK == 0
    t_len = seq + BLOCK
    nb = t_len // BLOCK
    n = batch * t_len

    h0 = jnp.concatenate([jnp.zeros((batch, PAD, d), x.dtype),
                          jnp.broadcast_to(meta_tokens.astype(x.dtype)[None], (batch, N_META, d)), x], axis=1)
    h32, hb = _ln_call(h0.reshape(n, d), ln_in_g, ln_in_b)

    pos = (jnp.arange(t_len) - PAD).astype(F32)
    inv_freq = ROPE_THETA ** (-jnp.arange(0, HEAD_DIM, 2, dtype=F32) / HEAD_DIM)
    ang = pos[:, None] * inv_freq[None, :]
    cos, sin = jnp.cos(ang), jnp.sin(ang)
    cos_t = jnp.tile(jnp.concatenate([cos, cos], axis=1), (1, LANES // HEAD_DIM))
    sin_t = jnp.tile(jnp.concatenate([-sin, sin], axis=1), (1, LANES // HEAD_DIM))

    lane = jnp.arange(LANES)
    partner = jnp.where(lane % HEAD_DIM < HEAD_DIM // 2, lane + HEAD_DIM // 2, lane - HEAD_DIM // 2)
    perm = (lane[:, None] == partner[None, :]).astype(BF16)

    head_of_lane = jnp.arange(D_INNER) // SSD_HEAD_DIM
    expand = (jnp.arange(LANES)[:, None] == head_of_lane[None, :]).astype(BF16)

    for l in range(DEPTH):
        proj = _proj_call(hb, _permute_w_in(w_in[l]))
        attn = _attn_call(proj, sinks[l], cos_t, sin_t, perm, batch, nb)
        yn = _ssd_call(proj, conv_w[l], conv_b[l], dt_bias[l], a_log[l], d_skip[l], ssd_norm_g[l], expand, batch, nb)
        h32, hb = _merge_call(attn, yn, proj, h32, w_attn_out[l].astype(BF16), w_ssd_out[l].astype(BF16),
                              w_o[l].astype(BF16), ln1_g[l], ln1_b[l])
        i = l // 2
        if l % 2 == 0:
            h32, hb = _ffn_call(hb, h32, ffn_wg[i].astype(BF16), ffn_wu[i].astype(BF16), ffn_wd[i].astype(BF16),
                                ln2_g[l], ln2_b[l])
        else:
            h32, hb = _moe_layer(h32, moe_router[i], moe_wg[i].astype(BF16), moe_wu[i].astype(BF16),
                                 moe_wd[i].astype(BF16), ln2_g[l], ln2_b[l])
    return h32.reshape(batch, t_len, d)[:, BLOCK:]
```

```python
import jax
import jax.numpy as jnp
from jax import lax
from jax.experimental import pallas as pl
from jax.experimental.pallas import tpu as pltpu

D_MODEL = 1024
DEPTH = 4
N_META = 16
BLOCK = 128
PAD = BLOCK - N_META
HEAD_DIM = 64
N_Q_HEADS = 16
N_KV_HEADS = 4
ROPE_THETA = 10000.0
ATTN_WIDTH = N_Q_HEADS * HEAD_DIM
KV_WIDTH = N_KV_HEADS * HEAD_DIM
D_INNER = 2 * D_MODEL
SSD_HEAD_DIM = 64
SSD_HEADS = D_INNER // SSD_HEAD_DIM
SSD_GROUPS = 4
SSD_STATE = 128
GROUP_WIDTH = D_INNER // SSD_GROUPS
CONV_WIDTH = 4
BC_WIDTH = SSD_GROUPS * SSD_STATE
D_FF = 2816
N_EXPERTS = 8
D_FF_EXPERT = 3584
ALPHA = (2 * DEPTH) ** 0.25
LN_EPS = 1e-5
RMS_EPS = 1e-5

LANES = 128
NEG_BIG = -1e30
VMEM_LIMIT = 56 * 1024 * 1024

OFF_Z = 0
OFF_XS = OFF_Z + D_INNER
OFF_GA = OFF_XS + D_INNER
OFF_GS = OFF_GA + D_MODEL
OFF_Q = OFF_GS + D_MODEL
OFF_B = OFF_Q + ATTN_WIDTH
OFF_C = OFF_B + BC_WIDTH
OFF_K = OFF_C + BC_WIDTH
OFF_V = OFF_K + KV_WIDTH
OFF_DT = OFF_V + KV_WIDTH
PROJ_WIDTH = OFF_DT + LANES
PROJ_TILE = PROJ_WIDTH // 3

F32 = jnp.float32
BF16 = jnp.bfloat16


def _pick_tile(n, candidates):
    for c in candidates:
        if n % c == 0:
            return c
    raise ValueError(f"no tile in {candidates} divides {n}")


def _params(*semantics):
    return pltpu.CompilerParams(dimension_semantics=semantics, vmem_limit_bytes=VMEM_LIMIT)


def _layer_norm(x, g, b):
    mu = jnp.mean(x, axis=-1, keepdims=True)
    xc = x - mu
    var = jnp.mean(xc * xc, axis=-1, keepdims=True)
    return xc * lax.rsqrt(var + LN_EPS) * g + b


def _silu(x):
    return x * jax.nn.sigmoid(x)


def _dot(a, b):
    return jnp.dot(a, b, preferred_element_type=F32)


def _dot_nt(a, b):
    return lax.dot_general(a, b, (((1,), (1,)), ((), ())), preferred_element_type=F32)


def _dot_tn(a, b):
    return lax.dot_general(a, b, (((0,), (0,)), ((), ())), preferred_element_type=F32)


def _split2(x):
    hi = x.astype(BF16)
    lo = (x - hi.astype(F32)).astype(BF16)
    return hi, lo


def _dot_exact_rhs(x, m):
    hi, lo = _split2(x)
    return _dot(hi, m) + _dot(lo, m)


def _ln_kernel(x_ref, g_ref, b_ref, o32_ref, o16_ref):
    y = _layer_norm(x_ref[...], g_ref[...], b_ref[...])
    o32_ref[...] = y
    o16_ref[...] = y.astype(BF16)


def _ln_call(x, g, b):
    n, d = x.shape
    tm = _pick_tile(n, (1280, 640, 512, 256, 128))
    row = pl.BlockSpec((tm, d), lambda i: (i, 0))
    vec = pl.BlockSpec((1, d), lambda i: (0, 0))
    return pl.pallas_call(
        _ln_kernel,
        grid=(n // tm,),
        in_specs=[row, vec, vec],
        out_specs=[row, row],
        out_shape=[jax.ShapeDtypeStruct((n, d), F32), jax.ShapeDtypeStruct((n, d), BF16)],
        compiler_params=_params("parallel"),
        name="ln_in",
    )(x, g.reshape(1, d), b.reshape(1, d))


def _proj_kernel(x_ref, w_ref, o_ref):
    o_ref[...] = _dot(x_ref[...], w_ref[...])


def _proj_call(hb, w):
    n, d = hb.shape
    width = w.shape[1]
    tm = _pick_tile(n, (1280, 640, 512, 256, 128))
    tn = PROJ_TILE
    return pl.pallas_call(
        _proj_kernel,
        grid=(n // tm, width // tn),
        in_specs=[pl.BlockSpec((tm, d), lambda i, j: (i, 0)), pl.BlockSpec((d, tn), lambda i, j: (0, j))],
        out_specs=pl.BlockSpec((tm, tn), lambda i, j: (i, j)),
        out_shape=jax.ShapeDtypeStruct((n, width), F32),
        compiler_params=_params("parallel", "arbitrary"),
        name="in_proj",
    )(hb, w)


def _half_split(t):
    lane = lax.broadcasted_iota(jnp.int32, t.shape, 1)
    low = lane < HEAD_DIM
    swapped = pltpu.roll(t, HEAD_DIM, 1)
    zero = jnp.zeros_like(t)
    a_lo = jnp.where(low, t, zero).astype(BF16)
    a_hi = jnp.where(low, zero, swapped).astype(BF16)
    b_lo = jnp.where(low, swapped, zero).astype(BF16)
    b_hi = jnp.where(low, zero, t).astype(BF16)
    return (a_lo, a_hi), (b_lo, b_hi)


def _attn_kernel(sink_ref, q_ref, k_ref, v_ref, cos_ref, sin_ref, perm_ref, o_ref, kop, vop, s_scr, e_scr):
    n = pl.program_id(1)

    @pl.when(n == 0)
    def _():
        kop[...] = jnp.zeros_like(kop)
        vop[...] = jnp.zeros_like(vop)

    cos, sin = cos_ref[...], sin_ref[...]
    perm = perm_ref[...]

    def rope(x):
        return x * cos + _dot(x.astype(BF16), perm) * sin

    for t in range(KV_WIDTH // LANES):
        ks = _half_split(rope(k_ref[:, t * LANES:(t + 1) * LANES]))
        vs = _half_split(v_ref[:, t * LANES:(t + 1) * LANES])
        for which in range(2):
            g = 2 * t + which
            for half in range(2):
                kop[g, half, 0:BLOCK, :] = kop[g, half, BLOCK:2 * BLOCK, :]
                vop[g, half, 0:BLOCK, :] = vop[g, half, BLOCK:2 * BLOCK, :]
                kop[g, half, BLOCK:2 * BLOCK, :] = ks[which][half]
                vop[g, half, BLOCK:2 * BLOCK, :] = vs[which][half]

    tiles_per_kv = (N_Q_HEADS // N_KV_HEADS) * HEAD_DIM // LANES
    jj = lax.broadcasted_iota(jnp.int32, (2 * BLOCK, BLOCK), 0)
    i = lax.broadcasted_iota(jnp.int32, (2 * BLOCK, BLOCK), 1)
    visible = (jj > i) & (jj <= i + BLOCK) & (jj + (n - 1) * BLOCK >= PAD)
    bias = jnp.tile(jnp.where(visible, 0.0, NEG_BIG).astype(F32), (1, tiles_per_kv))

    for g in range(N_KV_HEADS):
        t0 = g * tiles_per_kv
        qg = jnp.concatenate(
            [rope(q_ref[:, (t0 + t) * LANES:(t0 + t + 1) * LANES]).astype(BF16) for t in range(tiles_per_kv)], axis=0)
        for half in range(2):
            s_scr[2 * g + half] = _dot_nt(kop[g, half], qg)
    inv_denom = []
    for g in range(N_KV_HEADS):
        t0 = g * tiles_per_kv
        for half in range(2):
            s = s_scr[2 * g + half] + bias
            sink = jnp.concatenate(
                [jnp.full((1, BLOCK), sink_ref[2 * (t0 + t) + half], F32) for t in range(tiles_per_kv)], axis=1)
            m = jnp.maximum(jnp.max(s, axis=0, keepdims=True), sink)
            e = jnp.exp(s - m)
            inv_denom.append(1.0 / (jnp.sum(e, axis=0, keepdims=True) + jnp.exp(sink - m)))
            e_scr[2 * g + half] = e.astype(BF16)
    for g in range(N_KV_HEADS):
        t0 = g * tiles_per_kv
        acc = (_dot_tn(vop[g, 0], e_scr[2 * g]) * inv_denom[2 * g]
               + _dot_tn(vop[g, 1], e_scr[2 * g + 1]) * inv_denom[2 * g + 1])
        for t in range(tiles_per_kv):
            o_ref[(t0 + t) * LANES:(t0 + t + 1) * LANES, :] = acc[:, t * BLOCK:(t + 1) * BLOCK].astype(o_ref.dtype)


def _attn_call(proj, sinks, cos_t, sin_t, perm, batch, nb):
    n = proj.shape[0]

    def row(b, i):
        return b * nb + i

    kv_blk = OFF_K // KV_WIDTH
    in_specs = [
        pl.BlockSpec(memory_space=pltpu.SMEM),
        pl.BlockSpec((BLOCK, ATTN_WIDTH), lambda b, i: (row(b, i), OFF_Q // ATTN_WIDTH)),
        pl.BlockSpec((BLOCK, KV_WIDTH), lambda b, i: (row(b, i), kv_blk)),
        pl.BlockSpec((BLOCK, KV_WIDTH), lambda b, i: (row(b, i), kv_blk + 1)),
        pl.BlockSpec((BLOCK, LANES), lambda b, i: (i, 0)),
        pl.BlockSpec((BLOCK, LANES), lambda b, i: (i, 0)),
        pl.BlockSpec((LANES, LANES), lambda b, i: (0, 0)),
    ]
    operand = pltpu.VMEM((N_KV_HEADS, 2, 2 * BLOCK, LANES), BF16)
    return pl.pallas_call(
        _attn_kernel,
        grid=(batch, nb),
        in_specs=in_specs,
        out_specs=pl.BlockSpec((ATTN_WIDTH, BLOCK), lambda b, i: (0, row(b, i))),
        out_shape=jax.ShapeDtypeStruct((ATTN_WIDTH, n), BF16),
        scratch_shapes=[operand, operand,
                        pltpu.VMEM((2 * N_KV_HEADS, 2 * BLOCK, ATTN_WIDTH // N_KV_HEADS), F32),
                        pltpu.VMEM((2 * N_KV_HEADS, 2 * BLOCK, ATTN_WIDTH // N_KV_HEADS), BF16)],
        compiler_params=_params("parallel", "arbitrary"),
        name="swa_attn",
    )(sinks, proj, proj, proj, cos_t, sin_t, perm)


CARRY = 8


def _causal_conv(buf_ref, raw, w_ref, b_ref):
    buf_ref[CARRY:CARRY + BLOCK, :] = raw
    acc = b_ref[...] + w_ref[CONV_WIDTH - 1:CONV_WIDTH, :] * raw
    for j in range(CONV_WIDTH - 1):
        start = CARRY - (CONV_WIDTH - 1) + j
        acc = acc + w_ref[j:j + 1, :] * buf_ref[start:start + BLOCK, :]
    buf_ref[0:CARRY, :] = raw[BLOCK - CARRY:, :]
    return acc


def _ssd_kernel(xs_ref, z_ref, b_ref, c_ref, dt_ref, cwx_ref, cwb_ref, cwc_ref, cbx_ref, cbb_ref, cbc_ref,
                dtb_ref, alog_ref, dskip_ref, ng_ref, exp_ref, o_ref,
                bufx, bufb, bufc, state, y_scr):
    c = pl.program_id(1)

    @pl.when(c == 0)
    def _():
        state[...] = jnp.zeros_like(state)
        bufx[0:CARRY, :] = jnp.zeros((CARRY, D_INNER), F32)
        bufb[0:CARRY, :] = jnp.zeros((CARRY, BC_WIDTH), F32)
        bufc[0:CARRY, :] = jnp.zeros((CARRY, BC_WIDTH), F32)

    row = lax.broadcasted_iota(jnp.int32, (BLOCK, 1), 0)
    valid = ((row + c * BLOCK) >= PAD).astype(F32)

    xs = _silu(_causal_conv(bufx, xs_ref[...] * valid, cwx_ref, cbx_ref)) * valid
    bm = _silu(_causal_conv(bufb, b_ref[...] * valid, cwb_ref, cbb_ref)).astype(BF16)
    cm = _silu(_causal_conv(bufc, c_ref[...] * valid, cwc_ref, cbc_ref)).astype(BF16)

    dt = jnp.logaddexp(dt_ref[...] + dtb_ref[...], 0.0)
    da = dt * (-jnp.exp(alog_ref[...]))
    li = lax.broadcasted_iota(jnp.int32, (BLOCK, BLOCK), 0)
    si = lax.broadcasted_iota(jnp.int32, (BLOCK, BLOCK), 1)
    causal = li >= si
    tril = jnp.where(causal, 1.0, 0.0).astype(BF16)
    hi = da.astype(BF16)
    r1 = da - hi.astype(F32)
    mid = r1.astype(BF16)
    lo = (r1 - mid.astype(F32)).astype(BF16)
    a_cum = _dot(tril, hi) + _dot(tril, mid) + _dot(tril, lo)
    a_cum_t = a_cum.T
    a_last = a_cum[BLOCK - 1:BLOCK, :]
    exp_a = jnp.exp(a_cum)
    decay_to_end = jnp.exp(a_last - a_cum)

    expand = exp_ref[...]
    dt_x = _dot_exact_rhs(dt, expand)
    exp_a_x = _dot_exact_rhs(exp_a, expand)
    decay_x = _dot_exact_rhs(decay_to_end, expand)
    chunk_decay_x = exp_a_x[BLOCK - 1:BLOCK, :]

    xdt = xs * dt_x
    xdec_b = (xdt * decay_x).astype(BF16)

    lane = lax.broadcasted_iota(jnp.int32, (BLOCK, LANES), 1)
    low = lane < SSD_HEAD_DIM
    heads_per_group = SSD_HEADS // SSD_GROUPS
    for g in range(SSD_GROUPS):
        gs = slice(g * GROUP_WIDTH, (g + 1) * GROUP_WIDTH)
        bg = bm[:, g * SSD_STATE:(g + 1) * SSD_STATE]
        cg = cm[:, g * SSD_STATE:(g + 1) * SSD_STATE]
        cb = _dot_nt(cg, bg)
        bg_t = bg.astype(F32).T.astype(BF16)
        new_states = _dot(bg_t, xdec_b[:, gs])
        st = state[:, gs]
        y_off = _dot(cg, st.astype(BF16)) * exp_a_x[:, gs]
        state[:, gs] = st * chunk_decay_x[:, gs] + new_states
        for j in range(heads_per_group // 2):
            h0 = g * heads_per_group + 2 * j
            ms = []
            for h in (h0, h0 + 1):
                seg = a_cum[:, h:h + 1] - a_cum_t[h:h + 1, :]
                ms.append((cb * jnp.exp(jnp.where(causal, seg, NEG_BIG))).astype(BF16))
            lhs = jnp.concatenate(ms, axis=1)
            xp = xdt[:, h0 * SSD_HEAD_DIM:h0 * SSD_HEAD_DIM + LANES]
            zero = jnp.zeros_like(xp)
            rhs = jnp.concatenate([jnp.where(low, xp, zero), jnp.where(low, zero, xp)], axis=0).astype(BF16)
            y_scr[:, h0 * SSD_HEAD_DIM:h0 * SSD_HEAD_DIM + LANES] = (
                _dot(lhs, rhs) + y_off[:, 2 * j * SSD_HEAD_DIM:2 * j * SSD_HEAD_DIM + LANES])

    y = y_scr[...] + xs * dskip_ref[...]
    u = y * _silu(z_ref[...])
    for g in range(SSD_GROUPS):
        gs = slice(g * GROUP_WIDTH, (g + 1) * GROUP_WIDTH)
        ug = u[:, gs]
        scale = lax.rsqrt(jnp.mean(ug * ug, axis=-1, keepdims=True) + RMS_EPS)
        o_ref[:, gs] = (ug * scale * ng_ref[:, gs]).astype(o_ref.dtype)


def _ssd_call(proj, conv_w, conv_b, dt_bias, a_log, d_skip, norm_g, expand, batch, nc):
    n = proj.shape[0]

    def row(b, i):
        return b * nc + i

    def full(shape):
        return pl.BlockSpec(shape, lambda b, i: (0,) * len(shape))

    cwx, cwb, cwc = conv_w[:, :D_INNER], conv_w[:, D_INNER:D_INNER + BC_WIDTH], conv_w[:, D_INNER + BC_WIDTH:]
    cbx = conv_b[None, :D_INNER]
    cbb = conv_b[None, D_INNER:D_INNER + BC_WIDTH]
    cbc = conv_b[None, D_INNER + BC_WIDTH:]
    pad = LANES - SSD_HEADS
    dtb = jnp.pad(dt_bias, (0, pad))[None, :]
    alog = jnp.pad(a_log, (0, pad))[None, :]
    dskip = jnp.repeat(d_skip, SSD_HEAD_DIM)[None, :]
    in_specs = [
        pl.BlockSpec((BLOCK, D_INNER), lambda b, i: (row(b, i), OFF_XS // D_INNER)),
        pl.BlockSpec((BLOCK, D_INNER), lambda b, i: (row(b, i), OFF_Z // D_INNER)),
        pl.BlockSpec((BLOCK, BC_WIDTH), lambda b, i: (row(b, i), OFF_B // BC_WIDTH)),
        pl.BlockSpec((BLOCK, BC_WIDTH), lambda b, i: (row(b, i), OFF_C // BC_WIDTH)),
        pl.BlockSpec((BLOCK, LANES), lambda b, i: (row(b, i), OFF_DT // LANES)),
        full((CONV_WIDTH, D_INNER)), full((CONV_WIDTH, BC_WIDTH)), full((CONV_WIDTH, BC_WIDTH)),
        full((1, D_INNER)), full((1, BC_WIDTH)), full((1, BC_WIDTH)),
        full((1, LANES)), full((1, LANES)), full((1, D_INNER)), full((1, D_INNER)),
        full((LANES, D_INNER)),
    ]
    scratch = [
        pltpu.VMEM((CARRY + BLOCK, D_INNER), F32),
        pltpu.VMEM((CARRY + BLOCK, BC_WIDTH), F32),
        pltpu.VMEM((CARRY + BLOCK, BC_WIDTH), F32),
        pltpu.VMEM((SSD_STATE, D_INNER), F32),
        pltpu.VMEM((BLOCK, D_INNER), F32),
    ]
    return pl.pallas_call(
        _ssd_kernel,
        grid=(batch, nc),
        in_specs=in_specs,
        out_specs=pl.BlockSpec((BLOCK, D_INNER), lambda b, i: (row(b, i), 0)),
        out_shape=jax.ShapeDtypeStruct((n, D_INNER), BF16),
        scratch_shapes=scratch,
        compiler_params=_params("arbitrary", "arbitrary"),
        name="ssd",
    )(proj, proj, proj, proj, proj, cwx, cwb, cwc, cbx, cbb, cbc, dtb, alog, dskip, norm_g[None, :], expand)


def _merge_kernel(attn_ref, y_ref, ga_ref, gs_ref, h_ref, wa_ref, ws_ref, wo_ref, g_ref, b_ref, o32_ref, o16_ref):
    a = _dot_tn(attn_ref[...], wa_ref[...])
    s = _dot(y_ref[...], ws_ref[...])
    merged = jax.nn.sigmoid(ga_ref[...]) * a + jax.nn.sigmoid(gs_ref[...]) * s
    mix = _dot(merged.astype(BF16), wo_ref[...])
    y = _layer_norm(ALPHA * h_ref[...] + mix, g_ref[...], b_ref[...])
    o32_ref[...] = y
    o16_ref[...] = y.astype(BF16)


def _merge_call(attn, yn, proj, h32, wa, ws, wo, g, b):
    n, d = h32.shape
    tm = _pick_tile(n, (640, 512, 256, 128))
    row = pl.BlockSpec((tm, d), lambda i: (i, 0))
    vec = pl.BlockSpec((1, d), lambda i: (0, 0))

    def const(shape):
        return pl.BlockSpec(shape, lambda i: (0, 0), pipeline_mode=pl.Buffered(1))

    in_specs = [
        pl.BlockSpec((ATTN_WIDTH, tm), lambda i: (0, i)),
        pl.BlockSpec((tm, D_INNER), lambda i: (i, 0)),
        pl.BlockSpec((tm, D_MODEL), lambda i: (i, OFF_GA // D_MODEL)),
        pl.BlockSpec((tm, D_MODEL), lambda i: (i, OFF_GS // D_MODEL)),
        row,
        const((ATTN_WIDTH, d)), const((D_INNER, d)), const((d, d)),
        vec, vec,
    ]
    return pl.pallas_call(
        _merge_kernel,
        grid=(n // tm,),
        in_specs=in_specs,
        out_specs=[row, row],
        out_shape=[jax.ShapeDtypeStruct((n, d), F32), jax.ShapeDtypeStruct((n, d), BF16)],
        compiler_params=_params("parallel"),
        name="merge_out",
    )(attn, yn, proj, proj, h32, wa, ws, wo, g.reshape(1, d), b.reshape(1, d))


FF_CHUNK = 256


def _swiglu(xb, wg_ref, wu_ref, wd_ref):
    f = wg_ref.shape[1]
    acc = jnp.zeros((xb.shape[0], wd_ref.shape[1]), F32)
    for c in range(f // FF_CHUNK):
        cs = slice(c * FF_CHUNK, (c + 1) * FF_CHUNK)
        act = (_silu(_dot(xb, wg_ref[:, cs])) * _dot(xb, wu_ref[:, cs])).astype(BF16)
        acc = acc + _dot(act, wd_ref[cs, :])
    return acc


def _ffn_kernel(hb_ref, h_ref, wg_ref, wu_ref, wd_ref, g_ref, b_ref, o32_ref, o16_ref):
    f = _swiglu(hb_ref[...], wg_ref, wu_ref, wd_ref)
    y = _layer_norm(ALPHA * h_ref[...] + f, g_ref[...], b_ref[...])
    o32_ref[...] = y
    o16_ref[...] = y.astype(BF16)


def _ffn_call(hb, h32, wg, wu, wd, g, b):
    n, d = h32.shape
    f = wg.shape[1]
    assert f % FF_CHUNK == 0
    tm = _pick_tile(n, (640, 512, 256, 128))
    row = pl.BlockSpec((tm, d), lambda i: (i, 0))
    vec = pl.BlockSpec((1, d), lambda i: (0, 0))

    def const(shape):
        return pl.BlockSpec(shape, lambda i: (0, 0), pipeline_mode=pl.Buffered(1))

    return pl.pallas_call(
        _ffn_kernel,
        grid=(n // tm,),
        in_specs=[row, row, const((d, f)), const((d, f)), const((f, d)), vec, vec],
        out_specs=[row, row],
        out_shape=[jax.ShapeDtypeStruct((n, d), F32), jax.ShapeDtypeStruct((n, d), BF16)],
        compiler_params=_params("parallel"),
        name="dense_ffn",
    )(hb, h32, wg, wu, wd, g.reshape(1, d), b.reshape(1, d))


R_I1, R_I2, R_W1, R_W2, R_RANK1, R_RANK2 = range(6)


def _router_kernel(h_ref, w_ref, route_ref, count_ref, carry):
    @pl.when(pl.program_id(0) == 0)
    def _():
        carry[...] = jnp.zeros_like(carry)

    h = h_ref[...]
    tm = h.shape[0]
    lane = lax.broadcasted_iota(jnp.int32, (tm, LANES), 1)
    logits = jnp.full((tm, LANES), NEG_BIG, F32)
    for e in range(N_EXPERTS):
        logits = jnp.where(lane == e, jnp.sum(h * w_ref[e:e + 1, :], axis=-1, keepdims=True), logits)
    v1 = jnp.max(logits, axis=-1, keepdims=True)
    i1 = jnp.min(jnp.where(logits == v1, lane, LANES), axis=-1, keepdims=True)
    rest = jnp.where(lane == i1, NEG_BIG, logits)
    v2 = jnp.max(rest, axis=-1, keepdims=True)
    i2 = jnp.min(jnp.where(rest == v2, lane, LANES), axis=-1, keepdims=True)
    e2 = jnp.exp(v2 - v1)
    w1 = 1.0 / (1.0 + e2)
    w2 = e2 / (1.0 + e2)

    hot1 = jnp.where(lane == i1, 1.0, 0.0)
    hot2 = jnp.where(lane == i2, 1.0, 0.0)
    ri = lax.broadcasted_iota(jnp.int32, (tm, tm), 0)
    ci = lax.broadcasted_iota(jnp.int32, (tm, tm), 1)
    before = jnp.where(ri > ci, 1.0, 0.0).astype(BF16)
    base = carry[0:1, :]
    count1 = jnp.sum(hot1, axis=0, keepdims=True)
    rank1 = jnp.sum(hot1 * (_dot(before, hot1.astype(BF16)) + base), axis=-1, keepdims=True)
    rank2 = jnp.sum(hot2 * (_dot(before, hot2.astype(BF16)) + base + count1), axis=-1, keepdims=True)
    total = base + count1 + jnp.sum(hot2, axis=0, keepdims=True)
    carry[0:1, :] = total

    rec = jnp.zeros(logits.shape, F32)
    for slot, val in ((R_I1, i1.astype(F32)), (R_I2, i2.astype(F32)), (R_W1, w1), (R_W2, w2),
                      (R_RANK1, rank1), (R_RANK2, rank2)):
        rec = jnp.where(lane == slot, val, rec)
    route_ref[...] = rec
    count_ref[...] = jnp.broadcast_to(total, count_ref.shape)


def _router_call(h32, w_router):
    n, d = h32.shape
    tm = _pick_tile(n, (1280, 640, 512, 256, 128))
    w = w_router.T
    return pl.pallas_call(
        _router_kernel,
        grid=(n // tm,),
        in_specs=[pl.BlockSpec((tm, d), lambda i: (i, 0)), pl.BlockSpec((N_EXPERTS, d), lambda i: (0, 0))],
        out_specs=[pl.BlockSpec((tm, LANES), lambda i: (i, 0)), pl.BlockSpec((8, LANES), lambda i: (0, 0))],
        out_shape=[jax.ShapeDtypeStruct((n, LANES), F32), jax.ShapeDtypeStruct((8, LANES), F32)],
        scratch_shapes=[pltpu.VMEM((8, LANES), F32)],
        compiler_params=_params("arbitrary"),
        name="router",
    )(h32, w)


MOE_TILE = 512


def _row_copy(src, dst, src_row, dst_row, sem):
    return pltpu.make_async_copy(src.at[pl.ds(src_row, 1), :], dst.at[pl.ds(dst_row, 1), :], sem)


def _dispatch_kernel(pos_ref, h_ref, init_hbm, xs_hbm, sem):
    del init_hbm
    tm = h_ref.shape[0]

    def start(r, _):
        _row_copy(h_ref, xs_hbm, r, pos_ref[0, r], sem).start()
        _row_copy(h_ref, xs_hbm, r, pos_ref[0, tm + r], sem).start()
        return _

    lax.fori_loop(0, tm, start, 0)
    for _ in range(2):
        pltpu.make_async_copy(h_ref, xs_hbm.at[pl.ds(0, tm), :], sem).wait()


def _dispatch_call(pos, h32, n_rows):
    n, d = h32.shape
    nt, _, two_tm = pos.shape
    return pl.pallas_call(
        _dispatch_kernel,
        grid=(nt,),
        in_specs=[pl.BlockSpec((None, 1, two_tm), lambda i: (i, 0, 0), memory_space=pltpu.SMEM),
                  pl.BlockSpec((two_tm // 2, d), lambda i: (i, 0)), pl.BlockSpec(memory_space=pl.ANY)],
        out_specs=pl.BlockSpec(memory_space=pl.ANY),
        out_shape=jax.ShapeDtypeStruct((n_rows, d), F32),
        scratch_shapes=[pltpu.SemaphoreType.DMA(())],
        input_output_aliases={2: 0},
        compiler_params=_params("arbitrary"),
        name="moe_dispatch",
    )(pos, h32, jnp.zeros((n_rows, d), F32))


def _grouped_ffn_kernel(te_ref, na_ref, x_ref, wg_ref, wu_ref, wd_ref, o_ref):
    del te_ref
    i = pl.program_id(0)

    @pl.when(i < na_ref[0])
    def _():
        o_ref[...] = _swiglu(x_ref[...].astype(BF16), wg_ref, wu_ref, wd_ref)

    @pl.when(i >= na_ref[0])
    def _():
        o_ref[...] = jnp.zeros_like(o_ref)


def _grouped_ffn_call(tile_expert, n_active, xs, wg, wu, wd):
    n_rows, d = xs.shape
    f = wg.shape[2]
    tm = MOE_TILE
    grid_spec = pltpu.PrefetchScalarGridSpec(
        num_scalar_prefetch=2,
        grid=(n_rows // tm,),
        in_specs=[
            pl.BlockSpec((tm, d), lambda i, te, na: (i, 0)),
            pl.BlockSpec((None, d, f), lambda i, te, na: (te[i], 0, 0), pipeline_mode=pl.Buffered(1)),
            pl.BlockSpec((None, d, f), lambda i, te, na: (te[i], 0, 0), pipeline_mode=pl.Buffered(1)),
            pl.BlockSpec((None, f, d), lambda i, te, na: (te[i], 0, 0), pipeline_mode=pl.Buffered(1)),
        ],
        out_specs=pl.BlockSpec((tm, d), lambda i, te, na: (i, 0)),
    )
    return pl.pallas_call(
        _grouped_ffn_kernel,
        grid_spec=grid_spec,
        out_shape=jax.ShapeDtypeStruct((n_rows, d), F32),
        compiler_params=_params("arbitrary"),
        name="moe_grouped_ffn",
    )(tile_expert, n_active, xs, wg, wu, wd)


def _combine_kernel(pos_ref, next_pos_ref, ys_hbm, route_ref, h_ref, g_ref, b_ref, o32_ref, o16_ref, y1, y2, sem):
    i = pl.program_id(0)
    tm = y1.shape[1]
    slot = i % 2

    def gather(p_ref, s):
        def start(r, _):
            _row_copy(ys_hbm, y1.at[s], p_ref[0, r], r, sem.at[s]).start()
            _row_copy(ys_hbm, y2.at[s], p_ref[0, tm + r], r, sem.at[s]).start()
            return _

        lax.fori_loop(0, tm, start, 0)

    @pl.when(i == 0)
    def _():
        gather(pos_ref, 0)

    @pl.when(i + 1 < pl.num_programs(0))
    def _():
        gather(next_pos_ref, 1 - slot)

    pltpu.make_async_copy(ys_hbm.at[pl.ds(0, tm), :], y1.at[slot], sem.at[slot]).wait()
    pltpu.make_async_copy(ys_hbm.at[pl.ds(0, tm), :], y2.at[slot], sem.at[slot]).wait()
    route = route_ref[...]
    w1 = route[:, R_W1:R_W1 + 1]
    w2 = route[:, R_W2:R_W2 + 1]
    f = w1 * y1[slot] + w2 * y2[slot]
    y = _layer_norm(ALPHA * h_ref[...] + f, g_ref[...], b_ref[...])
    o32_ref[...] = y
    o16_ref[...] = y.astype(BF16)


def _combine_call(pos, ys, route, h32, g, b):
    n, d = h32.shape
    nt, _, two_tm = pos.shape
    tm = two_tm // 2
    row = pl.BlockSpec((tm, d), lambda i: (i, 0))
    vec = pl.BlockSpec((1, d), lambda i: (0, 0))
    return pl.pallas_call(
        _combine_kernel,
        grid=(nt,),
        in_specs=[pl.BlockSpec((None, 1, two_tm), lambda i: (i, 0, 0), memory_space=pltpu.SMEM),
                  pl.BlockSpec((None, 1, two_tm), lambda i: (jnp.minimum(i + 1, nt - 1), 0, 0),
                               memory_space=pltpu.SMEM),
                  pl.BlockSpec(memory_space=pl.ANY),
                  pl.BlockSpec((tm, LANES), lambda i: (i, 0)), row, vec, vec],
        out_specs=[row, row],
        out_shape=[jax.ShapeDtypeStruct((n, d), F32), jax.ShapeDtypeStruct((n, d), BF16)],
        scratch_shapes=[pltpu.VMEM((2, tm, d), F32), pltpu.VMEM((2, tm, d), F32), pltpu.SemaphoreType.DMA((2,))],
        compiler_params=_params("arbitrary"),
        name="moe_combine",
    )(pos, pos, ys, route, h32, g.reshape(1, d), b.reshape(1, d))


def _tiled_positions(pos1, pos2, tm):
    nt = pos1.shape[0] // tm
    return jnp.concatenate([pos1.reshape(nt, 1, tm), pos2.reshape(nt, 1, tm)], axis=2)


def _moe_layer(h32, w_router, wg, wu, wd, g, b):
    n, d = h32.shape
    tm = MOE_TILE
    n_rows = ((2 * n + N_EXPERTS * (tm - 1) + tm - 1) // tm) * tm
    n_tiles = n_rows // tm

    route, counts = _router_call(h32, w_router)
    count = counts[0, :N_EXPERTS].astype(jnp.int32)
    padded = ((count + tm - 1) // tm) * tm
    end = jnp.cumsum(padded)
    start = end - padded
    i1 = route[:, R_I1].astype(jnp.int32)
    i2 = route[:, R_I2].astype(jnp.int32)
    pos1 = start[i1] + route[:, R_RANK1].astype(jnp.int32)
    pos2 = start[i2] + route[:, R_RANK2].astype(jnp.int32)
    n_active = (end[-1] // tm).astype(jnp.int32)
    tile_start = jnp.arange(n_tiles, dtype=jnp.int32) * tm
    tile_expert = jnp.minimum(jnp.sum(tile_start[:, None] >= end[None, :], axis=1), N_EXPERTS - 1).astype(jnp.int32)
    tile_expert = jnp.where(tile_start < end[-1], tile_expert, tile_expert[jnp.maximum(n_active - 1, 0)])

    td = _pick_tile(n, (1280, 640, 512, 256, 128))
    xs = _dispatch_call(_tiled_positions(pos1, pos2, td), h32, n_rows)
    ys = _grouped_ffn_call(tile_expert, n_active.reshape(1), xs, wg, wu, wd)
    tc = _pick_tile(n, (512, 256, 128))
    return _combine_call(_tiled_positions(pos1, pos2, tc), ys, route, h32, g, b)


def _permute_w_in(w):
    sizes = (D_MODEL, D_MODEL, ATTN_WIDTH, KV_WIDTH, KV_WIDTH, D_INNER, D_INNER, BC_WIDTH, BC_WIDTH, SSD_HEADS)
    offs = [0]
    for s in sizes:
        offs.append(offs[-1] + s)
    g_a, g_s, q, k, v, z, xs, bm, cm, dt = (w[:, offs[i]:offs[i + 1]] for i in range(len(sizes)))
    q = q * (HEAD_DIM ** -0.5)
    tail = jnp.zeros((w.shape[0], PROJ_WIDTH - OFF_DT - SSD_HEADS), w.dtype)
    return jnp.concatenate([z, xs, g_a, g_s, q, bm, cm, k, v, dt, tail], axis=1).astype(BF16)


def kernel(x, meta_tokens, ln_in_g, ln_in_b, w_in, conv_w, conv_b, dt_bias, a_log, d_skip, ssd_norm_g, sinks,
           w_attn_out, w_ssd_out, w_o, ln1_g, ln1_b, ffn_wg, ffn_wu, ffn_wd, moe_router, moe_wg, moe_wu, moe_wd,
           ln2_g, ln2_b):
    batch, seq, d = x.shape
    assert d == D_MODEL and seq % BLOCK == 0
    t_len = seq + BLOCK
    nb = t_len // BLOCK
    n = batch * t_len

    h0 = jnp.concatenate([jnp.zeros((batch, PAD, d), x.dtype),
                          jnp.broadcast_to(meta_tokens.astype(x.dtype)[None], (batch, N_META, d)), x], axis=1)
    h32, hb = _ln_call(h0.reshape(n, d), ln_in_g, ln_in_b)

    pos = (jnp.arange(t_len) - PAD).astype(F32)
    inv_freq = ROPE_THETA ** (-jnp.arange(0, HEAD_DIM, 2, dtype=F32) / HEAD_DIM)
    ang = pos[:, None] * inv_freq[None, :]
    cos, sin = jnp.cos(ang), jnp.sin(ang)
    cos_t = jnp.tile(jnp.concatenate([cos, cos], axis=1), (1, LANES // HEAD_DIM))
    sin_t = jnp.tile(jnp.concatenate([-sin, sin], axis=1), (1, LANES // HEAD_DIM))

    lane = jnp.arange(LANES)
    partner = jnp.where(lane % HEAD_DIM < HEAD_DIM // 2, lane + HEAD_DIM // 2, lane - HEAD_DIM // 2)
    perm = (lane[:, None] == partner[None, :]).astype(BF16)

    head_of_lane = jnp.arange(D_INNER) // SSD_HEAD_DIM
    expand = (jnp.arange(LANES)[:, None] == head_of_lane[None, :]).astype(BF16)

    for l in range(DEPTH):
        proj = _proj_call(hb, _permute_w_in(w_in[l]))
        attn = _attn_call(proj, sinks[l], cos_t, sin_t, perm, batch, nb)
        yn = _ssd_call(proj, conv_w[l], conv_b[l], dt_bias[l], a_log[l], d_skip[l], ssd_norm_g[l], expand, batch, nb)
        h32, hb = _merge_call(attn, yn, proj, h32, w_attn_out[l].astype(BF16), w_ssd_out[l].astype(BF16),
                              w_o[l].astype(BF16), ln1_g[l], ln1_b[l])
        i = l // 2
        if l % 2 == 0:
            h32, hb = _ffn_call(hb, h32, ffn_wg[i].astype(BF16), ffn_wu[i].astype(BF16), ffn_wd[i].astype(BF16),
                                ln2_g[l], ln2_b[l])
        else:
            h32, hb = _moe_layer(h32, moe_router[i], moe_wg[i].astype(BF16), moe_wu[i].astype(BF16),
                                 moe_wd[i].astype(BF16), ln2_g[l], ln2_b[l])
    return h32.reshape(batch, t_len, d)[:, BLOCK:]
```

```python
import jax
import jax.numpy as jnp
from jax import lax
from jax.experimental import pallas as pl
from jax.experimental.pallas import tpu as pltpu

D_MODEL = 1024
DEPTH = 4
N_META = 16
BLOCK = 128
PAD = BLOCK - N_META
HEAD_DIM = 64
N_Q_HEADS = 16
N_KV_HEADS = 4
ROPE_THETA = 10000.0
ATTN_WIDTH = N_Q_HEADS * HEAD_DIM
KV_WIDTH = N_KV_HEADS * HEAD_DIM
D_INNER = 2 * D_MODEL
SSD_HEAD_DIM = 64
SSD_HEADS = D_INNER // SSD_HEAD_DIM
SSD_GROUPS = 4
SSD_STATE = 128
GROUP_WIDTH = D_INNER // SSD_GROUPS
CONV_WIDTH = 4
BC_WIDTH = SSD_GROUPS * SSD_STATE
D_FF = 2816
N_EXPERTS = 8
D_FF_EXPERT = 3584
ALPHA = (2 * DEPTH) ** 0.25
LN_EPS = 1e-5
RMS_EPS = 1e-5

LANES = 128
NEG_BIG = -1e30
VMEM_LIMIT = 56 * 1024 * 1024

OFF_Z = 0
OFF_XS = OFF_Z + D_INNER
OFF_GA = OFF_XS + D_INNER
OFF_GS = OFF_GA + D_MODEL
OFF_Q = OFF_GS + D_MODEL
OFF_B = OFF_Q + ATTN_WIDTH
OFF_C = OFF_B + BC_WIDTH
OFF_K = OFF_C + BC_WIDTH
OFF_V = OFF_K + KV_WIDTH
OFF_DT = OFF_V + KV_WIDTH
PROJ_WIDTH = OFF_DT + LANES
PROJ_TILE = PROJ_WIDTH // 3

F32 = jnp.float32
BF16 = jnp.bfloat16


def _pick_tile(n, candidates):
    for c in candidates:
        if n % c == 0:
            return c
    raise ValueError(f"no tile in {candidates} divides {n}")


def _params(*semantics):
    return pltpu.CompilerParams(dimension_semantics=semantics, vmem_limit_bytes=VMEM_LIMIT)


def _layer_norm(x, g, b):
    mu = jnp.mean(x, axis=-1, keepdims=True)
    xc = x - mu
    var = jnp.mean(xc * xc, axis=-1, keepdims=True)
    return xc * lax.rsqrt(var + LN_EPS) * g + b


def _silu(x):
    half = 0.5 * x
    return half + half * jnp.tanh(half)


def _dot(a, b):
    return jnp.dot(a, b, preferred_element_type=F32)


def _dot_nt(a, b):
    return lax.dot_general(a, b, (((1,), (1,)), ((), ())), preferred_element_type=F32)


def _dot_tn(a, b):
    return lax.dot_general(a, b, (((0,), (0,)), ((), ())), preferred_element_type=F32)


def _split2(x):
    hi = x.astype(BF16)
    lo = (x - hi.astype(F32)).astype(BF16)
    return hi, lo


def _dot_exact_rhs(x, m):
    hi, lo = _split2(x)
    return _dot(hi, m) + _dot(lo, m)


def _ln_kernel(x_ref, g_ref, b_ref, o32_ref, o16_ref):
    y = _layer_norm(x_ref[...], g_ref[...], b_ref[...])
    o32_ref[...] = y
    o16_ref[...] = y.astype(BF16)


def _ln_call(x, g, b):
    n, d = x.shape
    tm = _pick_tile(n, (1280, 640, 512, 256, 128))
    row = pl.BlockSpec((tm, d), lambda i: (i, 0))
    vec = pl.BlockSpec((1, d), lambda i: (0, 0))
    return pl.pallas_call(
        _ln_kernel,
        grid=(n // tm,),
        in_specs=[row, vec, vec],
        out_specs=[row, row],
        out_shape=[jax.ShapeDtypeStruct((n, d), F32), jax.ShapeDtypeStruct((n, d), BF16)],
        compiler_params=_params("parallel"),
        name="ln_in",
    )(x, g.reshape(1, d), b.reshape(1, d))


def _proj_kernel(x_ref, w_ref, o_ref):
    o_ref[...] = _dot(x_ref[...], w_ref[...])


def _proj_call(hb, w):
    n, d = hb.shape
    width = w.shape[1]
    tm = _pick_tile(n, (1280, 640, 512, 256, 128))
    tn = PROJ_TILE
    return pl.pallas_call(
        _proj_kernel,
        grid=(n // tm, width // tn),
        in_specs=[pl.BlockSpec((tm, d), lambda i, j: (i, 0)), pl.BlockSpec((d, tn), lambda i, j: (0, j))],
        out_specs=pl.BlockSpec((tm, tn), lambda i, j: (i, j)),
        out_shape=jax.ShapeDtypeStruct((n, width), F32),
        compiler_params=_params("parallel", "arbitrary"),
        name="in_proj",
    )(hb, w)


def _half_split(t):
    lane = lax.broadcasted_iota(jnp.int32, t.shape, 1)
    low = lane < HEAD_DIM
    swapped = pltpu.roll(t, HEAD_DIM, 1)
    zero = jnp.zeros_like(t)
    a_lo = jnp.where(low, t, zero).astype(BF16)
    a_hi = jnp.where(low, zero, swapped).astype(BF16)
    b_lo = jnp.where(low, swapped, zero).astype(BF16)
    b_hi = jnp.where(low, zero, t).astype(BF16)
    return (a_lo, a_hi), (b_lo, b_hi)


def _attn_kernel(sink_ref, q_ref, k_ref, v_ref, cos_ref, sin_ref, perm_ref, o_ref, kop, vop, s_scr, e_scr):
    n = pl.program_id(1)

    @pl.when(n == 0)
    def _():
        kop[...] = jnp.zeros_like(kop)
        vop[...] = jnp.zeros_like(vop)

    cos, sin = cos_ref[...], sin_ref[...]
    perm = perm_ref[...]

    def rope(x):
        return x * cos + _dot(x.astype(BF16), perm) * sin

    for t in range(KV_WIDTH // LANES):
        ks = _half_split(rope(k_ref[:, t * LANES:(t + 1) * LANES]))
        vs = _half_split(v_ref[:, t * LANES:(t + 1) * LANES])
        for which in range(2):
            g = 2 * t + which
            for half in range(2):
                kop[g, half, 0:BLOCK, :] = kop[g, half, BLOCK:2 * BLOCK, :]
                vop[g, half, 0:BLOCK, :] = vop[g, half, BLOCK:2 * BLOCK, :]
                kop[g, half, BLOCK:2 * BLOCK, :] = ks[which][half]
                vop[g, half, BLOCK:2 * BLOCK, :] = vs[which][half]

    tiles_per_kv = (N_Q_HEADS // N_KV_HEADS) * HEAD_DIM // LANES
    jj = lax.broadcasted_iota(jnp.int32, (2 * BLOCK, BLOCK), 0)
    i = lax.broadcasted_iota(jnp.int32, (2 * BLOCK, BLOCK), 1)
    visible = (jj > i) & (jj <= i + BLOCK) & (jj + (n - 1) * BLOCK >= PAD)
    bias = jnp.tile(jnp.where(visible, 0.0, NEG_BIG).astype(F32), (1, tiles_per_kv))

    for g in range(N_KV_HEADS):
        t0 = g * tiles_per_kv
        qg = jnp.concatenate(
            [rope(q_ref[:, (t0 + t) * LANES:(t0 + t + 1) * LANES]).astype(BF16) for t in range(tiles_per_kv)], axis=0)
        for half in range(2):
            s_scr[2 * g + half] = _dot_nt(kop[g, half], qg)
    inv_denom = []
    for g in range(N_KV_HEADS):
        t0 = g * tiles_per_kv
        for half in range(2):
            s = s_scr[2 * g + half] + bias
            sink = jnp.concatenate(
                [jnp.full((1, BLOCK), sink_ref[2 * (t0 + t) + half], F32) for t in range(tiles_per_kv)], axis=1)
            m = jnp.maximum(jnp.max(s, axis=0, keepdims=True), sink)
            e = jnp.exp(s - m)
            inv_denom.append(1.0 / (jnp.sum(e, axis=0, keepdims=True) + jnp.exp(sink - m)))
            e_scr[2 * g + half] = e.astype(BF16)
    for g in range(N_KV_HEADS):
        t0 = g * tiles_per_kv
        acc = (_dot_tn(vop[g, 0], e_scr[2 * g]) * inv_denom[2 * g]
               + _dot_tn(vop[g, 1], e_scr[2 * g + 1]) * inv_denom[2 * g + 1])
        for t in range(tiles_per_kv):
            o_ref[(t0 + t) * LANES:(t0 + t + 1) * LANES, :] = acc[:, t * BLOCK:(t + 1) * BLOCK].astype(o_ref.dtype)


def _attn_call(proj, sinks, cos_t, sin_t, perm, batch, nb):
    n = proj.shape[0]

    def row(b, i):
        return b * nb + i

    kv_blk = OFF_K // KV_WIDTH
    in_specs = [
        pl.BlockSpec(memory_space=pltpu.SMEM),
        pl.BlockSpec((BLOCK, ATTN_WIDTH), lambda b, i: (row(b, i), OFF_Q // ATTN_WIDTH)),
        pl.BlockSpec((BLOCK, KV_WIDTH), lambda b, i: (row(b, i), kv_blk)),
        pl.BlockSpec((BLOCK, KV_WIDTH), lambda b, i: (row(b, i), kv_blk + 1)),
        pl.BlockSpec((BLOCK, LANES), lambda b, i: (i, 0)),
        pl.BlockSpec((BLOCK, LANES), lambda b, i: (i, 0)),
        pl.BlockSpec((LANES, LANES), lambda b, i: (0, 0)),
    ]
    operand = pltpu.VMEM((N_KV_HEADS, 2, 2 * BLOCK, LANES), BF16)
    return pl.pallas_call(
        _attn_kernel,
        grid=(batch, nb),
        in_specs=in_specs,
        out_specs=pl.BlockSpec((ATTN_WIDTH, BLOCK), lambda b, i: (0, row(b, i))),
        out_shape=jax.ShapeDtypeStruct((ATTN_WIDTH, n), BF16),
        scratch_shapes=[operand, operand,
                        pltpu.VMEM((2 * N_KV_HEADS, 2 * BLOCK, ATTN_WIDTH // N_KV_HEADS), F32),
                        pltpu.VMEM((2 * N_KV_HEADS, 2 * BLOCK, ATTN_WIDTH // N_KV_HEADS), BF16)],
        compiler_params=_params("parallel", "arbitrary"),
        name="swa_attn",
    )(sinks, proj, proj, proj, cos_t, sin_t, perm)


CARRY = 8


def _causal_conv(buf_ref, raw, w_ref, b_ref):
    buf_ref[CARRY:CARRY + BLOCK, :] = raw
    acc = b_ref[...] + w_ref[CONV_WIDTH - 1:CONV_WIDTH, :] * raw
    for j in range(CONV_WIDTH - 1):
        start = CARRY - (CONV_WIDTH - 1) + j
        acc = acc + w_ref[j:j + 1, :] * buf_ref[start:start + BLOCK, :]
    buf_ref[0:CARRY, :] = raw[BLOCK - CARRY:, :]
    return acc


def _ssd_kernel(xs_ref, z_ref, b_ref, c_ref, dt_ref, cwx_ref, cwb_ref, cwc_ref, cbx_ref, cbb_ref, cbc_ref,
                dtb_ref, alog_ref, dskip_ref, ng_ref, exp_ref, o_ref,
                bufx, bufb, bufc, state, y_scr):
    c = pl.program_id(1)

    @pl.when(c == 0)
    def _():
        state[...] = jnp.zeros_like(state)
        bufx[0:CARRY, :] = jnp.zeros((CARRY, D_INNER), F32)
        bufb[0:CARRY, :] = jnp.zeros((CARRY, BC_WIDTH), F32)
        bufc[0:CARRY, :] = jnp.zeros((CARRY, BC_WIDTH), F32)

    row = lax.broadcasted_iota(jnp.int32, (BLOCK, 1), 0)
    valid = ((row + c * BLOCK) >= PAD).astype(F32)

    xs = _silu(_causal_conv(bufx, xs_ref[...] * valid, cwx_ref, cbx_ref)) * valid
    bm = _silu(_causal_conv(bufb, b_ref[...] * valid, cwb_ref, cbb_ref)).astype(BF16)
    cm = _silu(_causal_conv(bufc, c_ref[...] * valid, cwc_ref, cbc_ref)).astype(BF16)

    dt = jnp.logaddexp(dt_ref[...] + dtb_ref[...], 0.0)
    da = dt * (-jnp.exp(alog_ref[...]))
    li = lax.broadcasted_iota(jnp.int32, (BLOCK, BLOCK), 0)
    si = lax.broadcasted_iota(jnp.int32, (BLOCK, BLOCK), 1)
    causal = li >= si
    tril = jnp.where(causal, 1.0, 0.0).astype(BF16)
    hi = da.astype(BF16)
    r1 = da - hi.astype(F32)
    mid = r1.astype(BF16)
    lo = (r1 - mid.astype(F32)).astype(BF16)
    a_cum = _dot(tril, hi) + _dot(tril, mid) + _dot(tril, lo)
    a_cum_t = a_cum.T
    a_last = a_cum[BLOCK - 1:BLOCK, :]
    exp_a = jnp.exp(a_cum)
    decay_to_end = jnp.exp(a_last - a_cum)

    expand = exp_ref[...]
    dt_x = _dot_exact_rhs(dt, expand)
    exp_a_x = _dot_exact_rhs(exp_a, expand)
    decay_x = _dot_exact_rhs(decay_to_end, expand)
    chunk_decay_x = exp_a_x[BLOCK - 1:BLOCK, :]

    xdt = xs * dt_x
    xdec_b = (xdt * decay_x).astype(BF16)

    lane = lax.broadcasted_iota(jnp.int32, (BLOCK, LANES), 1)
    low = lane < SSD_HEAD_DIM
    heads_per_group = SSD_HEADS // SSD_GROUPS
    for g in range(SSD_GROUPS):
        gs = slice(g * GROUP_WIDTH, (g + 1) * GROUP_WIDTH)
        bg = bm[:, g * SSD_STATE:(g + 1) * SSD_STATE]
        cg = cm[:, g * SSD_STATE:(g + 1) * SSD_STATE]
        cb = _dot_nt(cg, bg)
        bg_t = bg.astype(F32).T.astype(BF16)
        new_states = _dot(bg_t, xdec_b[:, gs])
        st = state[:, gs]
        y_off = _dot(cg, st.astype(BF16)) * exp_a_x[:, gs]
        state[:, gs] = st * chunk_decay_x[:, gs] + new_states
        for j in range(heads_per_group // 2):
            h0 = g * heads_per_group + 2 * j
            ms = []
            for h in (h0, h0 + 1):
                seg = a_cum[:, h:h + 1] - a_cum_t[h:h + 1, :]
                ms.append((cb * jnp.exp(jnp.where(causal, seg, NEG_BIG))).astype(BF16))
            lhs = jnp.concatenate(ms, axis=1)
            xp = xdt[:, h0 * SSD_HEAD_DIM:h0 * SSD_HEAD_DIM + LANES]
            zero = jnp.zeros_like(xp)
            rhs = jnp.concatenate([jnp.where(low, xp, zero), jnp.where(low, zero, xp)], axis=0).astype(BF16)
            y_scr[:, h0 * SSD_HEAD_DIM:h0 * SSD_HEAD_DIM + LANES] = (
                _dot(lhs, rhs) + y_off[:, 2 * j * SSD_HEAD_DIM:2 * j * SSD_HEAD_DIM + LANES])

    y = y_scr[...] + xs * dskip_ref[...]
    u = y * _silu(z_ref[...])
    for g in range(SSD_GROUPS):
        gs = slice(g * GROUP_WIDTH, (g + 1) * GROUP_WIDTH)
        ug = u[:, gs]
        scale = lax.rsqrt(jnp.mean(ug * ug, axis=-1, keepdims=True) + RMS_EPS)
        o_ref[:, gs] = (ug * scale * ng_ref[:, gs]).astype(o_ref.dtype)


def _ssd_call(proj, conv_w, conv_b, dt_bias, a_log, d_skip, norm_g, expand, batch, nc):
    n = proj.shape[0]

    def row(b, i):
        return b * nc + i

    def full(shape):
        return pl.BlockSpec(shape, lambda b, i: (0,) * len(shape))

    cwx, cwb, cwc = conv_w[:, :D_INNER], conv_w[:, D_INNER:D_INNER + BC_WIDTH], conv_w[:, D_INNER + BC_WIDTH:]
    cbx = conv_b[None, :D_INNER]
    cbb = conv_b[None, D_INNER:D_INNER + BC_WIDTH]
    cbc = conv_b[None, D_INNER + BC_WIDTH:]
    pad = LANES - SSD_HEADS
    dtb = jnp.pad(dt_bias, (0, pad))[None, :]
    alog = jnp.pad(a_log, (0, pad))[None, :]
    dskip = jnp.repeat(d_skip, SSD_HEAD_DIM)[None, :]
    in_specs = [
        pl.BlockSpec((BLOCK, D_INNER), lambda b, i: (row(b, i), OFF_XS // D_INNER)),
        pl.BlockSpec((BLOCK, D_INNER), lambda b, i: (row(b, i), OFF_Z // D_INNER)),
        pl.BlockSpec((BLOCK, BC_WIDTH), lambda b, i: (row(b, i), OFF_B // BC_WIDTH)),
        pl.BlockSpec((BLOCK, BC_WIDTH), lambda b, i: (row(b, i), OFF_C // BC_WIDTH)),
        pl.BlockSpec((BLOCK, LANES), lambda b, i: (row(b, i), OFF_DT // LANES)),
        full((CONV_WIDTH, D_INNER)), full((CONV_WIDTH, BC_WIDTH)), full((CONV_WIDTH, BC_WIDTH)),
        full((1, D_INNER)), full((1, BC_WIDTH)), full((1, BC_WIDTH)),
        full((1, LANES)), full((1, LANES)), full((1, D_INNER)), full((1, D_INNER)),
        full((LANES, D_INNER)),
    ]
    scratch = [
        pltpu.VMEM((CARRY + BLOCK, D_INNER), F32),
        pltpu.VMEM((CARRY + BLOCK, BC_WIDTH), F32),
        pltpu.VMEM((CARRY + BLOCK, BC_WIDTH), F32),
        pltpu.VMEM((SSD_STATE, D_INNER), F32),
        pltpu.VMEM((BLOCK, D_INNER), F32),
    ]
    return pl.pallas_call(
        _ssd_kernel,
        grid=(batch, nc),
        in_specs=in_specs,
        out_specs=pl.BlockSpec((BLOCK, D_INNER), lambda b, i: (row(b, i), 0)),
        out_shape=jax.ShapeDtypeStruct((n, D_INNER), BF16),
        scratch_shapes=scratch,
        compiler_params=_params("arbitrary", "arbitrary"),
        name="ssd",
    )(proj, proj, proj, proj, proj, cwx, cwb, cwc, cbx, cbb, cbc, dtb, alog, dskip, norm_g[None, :], expand)


def _merge_kernel(attn_ref, y_ref, ga_ref, gs_ref, h_ref, wa_ref, ws_ref, wo_ref, g_ref, b_ref, o32_ref, o16_ref):
    a = _dot_tn(attn_ref[...], wa_ref[...])
    s = _dot(y_ref[...], ws_ref[...])
    merged = jax.nn.sigmoid(ga_ref[...]) * a + jax.nn.sigmoid(gs_ref[...]) * s
    mix = _dot(merged.astype(BF16), wo_ref[...])
    y = _layer_norm(ALPHA * h_ref[...] + mix, g_ref[...], b_ref[...])
    o32_ref[...] = y
    o16_ref[...] = y.astype(BF16)


def _merge_call(attn, yn, proj, h32, wa, ws, wo, g, b):
    n, d = h32.shape
    tm = _pick_tile(n, (640, 512, 256, 128))
    row = pl.BlockSpec((tm, d), lambda i: (i, 0))
    vec = pl.BlockSpec((1, d), lambda i: (0, 0))

    def const(shape):
        return pl.BlockSpec(shape, lambda i: (0, 0), pipeline_mode=pl.Buffered(1))

    in_specs = [
        pl.BlockSpec((ATTN_WIDTH, tm), lambda i: (0, i)),
        pl.BlockSpec((tm, D_INNER), lambda i: (i, 0)),
        pl.BlockSpec((tm, D_MODEL), lambda i: (i, OFF_GA // D_MODEL)),
        pl.BlockSpec((tm, D_MODEL), lambda i: (i, OFF_GS // D_MODEL)),
        row,
        const((ATTN_WIDTH, d)), const((D_INNER, d)), const((d, d)),
        vec, vec,
    ]
    return pl.pallas_call(
        _merge_kernel,
        grid=(n // tm,),
        in_specs=in_specs,
        out_specs=[row, row],
        out_shape=[jax.ShapeDtypeStruct((n, d), F32), jax.ShapeDtypeStruct((n, d), BF16)],
        compiler_params=_params("parallel"),
        name="merge_out",
    )(attn, yn, proj, proj, h32, wa, ws, wo, g.reshape(1, d), b.reshape(1, d))


FF_CHUNK = 256


def _swiglu(xb, wg_ref, wu_ref, wd_ref):
    f = wg_ref.shape[1]
    acc = jnp.zeros((xb.shape[0], wd_ref.shape[1]), F32)
    for c in range(f // FF_CHUNK):
        cs = slice(c * FF_CHUNK, (c + 1) * FF_CHUNK)
        act = (_silu(_dot(xb, wg_ref[:, cs])) * _dot(xb, wu_ref[:, cs])).astype(BF16)
        acc = acc + _dot(act, wd_ref[cs, :])
    return acc


def _ffn_kernel(hb_ref, h_ref, wg_ref, wu_ref, wd_ref, g_ref, b_ref, o32_ref, o16_ref):
    f = _swiglu(hb_ref[...], wg_ref, wu_ref, wd_ref)
    y = _layer_norm(ALPHA * h_ref[...] + f, g_ref[...], b_ref[...])
    o32_ref[...] = y
    o16_ref[...] = y.astype(BF16)


def _ffn_call(hb, h32, wg, wu, wd, g, b):
    n, d = h32.shape
    f = wg.shape[1]
    assert f % FF_CHUNK == 0
    tm = _pick_tile(n, (640, 512, 256, 128))
    row = pl.BlockSpec((tm, d), lambda i: (i, 0))
    vec = pl.BlockSpec((1, d), lambda i: (0, 0))

    def const(shape):
        return pl.BlockSpec(shape, lambda i: (0, 0), pipeline_mode=pl.Buffered(1))

    return pl.pallas_call(
        _ffn_kernel,
        grid=(n // tm,),
        in_specs=[row, row, const((d, f)), const((d, f)), const((f, d)), vec, vec],
        out_specs=[row, row],
        out_shape=[jax.ShapeDtypeStruct((n, d), F32), jax.ShapeDtypeStruct((n, d), BF16)],
        compiler_params=_params("parallel"),
        name="dense_ffn",
    )(hb, h32, wg, wu, wd, g.reshape(1, d), b.reshape(1, d))


R_I1, R_I2, R_W1, R_W2, R_RANK1, R_RANK2 = range(6)


def _router_kernel(h_ref, w_ref, route_ref, count_ref, carry):
    @pl.when(pl.program_id(0) == 0)
    def _():
        carry[...] = jnp.zeros_like(carry)

    h = h_ref[...]
    tm = h.shape[0]
    lane = lax.broadcasted_iota(jnp.int32, (tm, LANES), 1)
    logits = jnp.full((tm, LANES), NEG_BIG, F32)
    for e in range(N_EXPERTS):
        logits = jnp.where(lane == e, jnp.sum(h * w_ref[e:e + 1, :], axis=-1, keepdims=True), logits)
    v1 = jnp.max(logits, axis=-1, keepdims=True)
    i1 = jnp.min(jnp.where(logits == v1, lane, LANES), axis=-1, keepdims=True)
    rest = jnp.where(lane == i1, NEG_BIG, logits)
    v2 = jnp.max(rest, axis=-1, keepdims=True)
    i2 = jnp.min(jnp.where(rest == v2, lane, LANES), axis=-1, keepdims=True)
    e2 = jnp.exp(v2 - v1)
    w1 = 1.0 / (1.0 + e2)
    w2 = e2 / (1.0 + e2)

    hot1 = jnp.where(lane == i1, 1.0, 0.0)
    hot2 = jnp.where(lane == i2, 1.0, 0.0)
    ri = lax.broadcasted_iota(jnp.int32, (tm, tm), 0)
    ci = lax.broadcasted_iota(jnp.int32, (tm, tm), 1)
    before = jnp.where(ri > ci, 1.0, 0.0).astype(BF16)
    base = carry[0:1, :]
    count1 = jnp.sum(hot1, axis=0, keepdims=True)
    rank1 = jnp.sum(hot1 * (_dot(before, hot1.astype(BF16)) + base), axis=-1, keepdims=True)
    rank2 = jnp.sum(hot2 * (_dot(before, hot2.astype(BF16)) + base + count1), axis=-1, keepdims=True)
    total = base + count1 + jnp.sum(hot2, axis=0, keepdims=True)
    carry[0:1, :] = total

    rec = jnp.zeros(logits.shape, F32)
    for slot, val in ((R_I1, i1.astype(F32)), (R_I2, i2.astype(F32)), (R_W1, w1), (R_W2, w2),
                      (R_RANK1, rank1), (R_RANK2, rank2)):
        rec = jnp.where(lane == slot, val, rec)
    route_ref[...] = rec
    count_ref[...] = jnp.broadcast_to(total, count_ref.shape)


def _router_call(h32, w_router):
    n, d = h32.shape
    tm = _pick_tile(n, (1280, 640, 512, 256, 128))
    w = w_router.T
    return pl.pallas_call(
        _router_kernel,
        grid=(n // tm,),
        in_specs=[pl.BlockSpec((tm, d), lambda i: (i, 0)), pl.BlockSpec((N_EXPERTS, d), lambda i: (0, 0))],
        out_specs=[pl.BlockSpec((tm, LANES), lambda i: (i, 0)), pl.BlockSpec((8, LANES), lambda i: (0, 0))],
        out_shape=[jax.ShapeDtypeStruct((n, LANES), F32), jax.ShapeDtypeStruct((8, LANES), F32)],
        scratch_shapes=[pltpu.VMEM((8, LANES), F32)],
        compiler_params=_params("arbitrary"),
        name="router",
    )(h32, w)


MOE_TILE = 512


def _row_copy(src, dst, src_row, dst_row, sem):
    return pltpu.make_async_copy(src.at[pl.ds(src_row, 1), :], dst.at[pl.ds(dst_row, 1), :], sem)


def _dispatch_kernel(pos_ref, h_ref, init_hbm, xs_hbm, sem):
    del init_hbm
    tm = h_ref.shape[0]

    def start(r, _):
        _row_copy(h_ref, xs_hbm, r, pos_ref[0, r], sem).start()
        _row_copy(h_ref, xs_hbm, r, pos_ref[0, tm + r], sem).start()
        return _

    lax.fori_loop(0, tm, start, 0, unroll=8)
    for _ in range(2):
        pltpu.make_async_copy(h_ref, xs_hbm.at[pl.ds(0, tm), :], sem).wait()


def _dispatch_call(pos, h32, n_rows):
    n, d = h32.shape
    nt, _, two_tm = pos.shape
    return pl.pallas_call(
        _dispatch_kernel,
        grid=(nt,),
        in_specs=[pl.BlockSpec((None, 1, two_tm), lambda i: (i, 0, 0), memory_space=pltpu.SMEM),
                  pl.BlockSpec((two_tm // 2, d), lambda i: (i, 0)), pl.BlockSpec(memory_space=pl.ANY)],
        out_specs=pl.BlockSpec(memory_space=pl.ANY),
        out_shape=jax.ShapeDtypeStruct((n_rows, d), F32),
        scratch_shapes=[pltpu.SemaphoreType.DMA(())],
        input_output_aliases={2: 0},
        compiler_params=_params("arbitrary"),
        name="moe_dispatch",
    )(pos, h32, jnp.zeros((n_rows, d), F32))


def _grouped_ffn_kernel(te_ref, na_ref, x_ref, wg_ref, wu_ref, wd_ref, o_ref):
    del te_ref
    i = pl.program_id(0)

    @pl.when(i < na_ref[0])
    def _():
        o_ref[...] = _swiglu(x_ref[...].astype(BF16), wg_ref, wu_ref, wd_ref)

    @pl.when(i >= na_ref[0])
    def _():
        o_ref[...] = jnp.zeros_like(o_ref)


def _grouped_ffn_call(tile_expert, n_active, xs, wg, wu, wd):
    n_rows, d = xs.shape
    f = wg.shape[2]
    tm = MOE_TILE
    grid_spec = pltpu.PrefetchScalarGridSpec(
        num_scalar_prefetch=2,
        grid=(n_rows // tm,),
        in_specs=[
            pl.BlockSpec((tm, d), lambda i, te, na: (i, 0)),
            pl.BlockSpec((None, d, f), lambda i, te, na: (te[i], 0, 0), pipeline_mode=pl.Buffered(1)),
            pl.BlockSpec((None, d, f), lambda i, te, na: (te[i], 0, 0), pipeline_mode=pl.Buffered(1)),
            pl.BlockSpec((None, f, d), lambda i, te, na: (te[i], 0, 0), pipeline_mode=pl.Buffered(1)),
        ],
        out_specs=pl.BlockSpec((tm, d), lambda i, te, na: (i, 0)),
    )
    return pl.pallas_call(
        _grouped_ffn_kernel,
        grid_spec=grid_spec,
        out_shape=jax.ShapeDtypeStruct((n_rows, d), F32),
        compiler_params=_params("arbitrary"),
        name="moe_grouped_ffn",
    )(tile_expert, n_active, xs, wg, wu, wd)


def _combine_kernel(pos_ref, next_pos_ref, ys_hbm, route_ref, h_ref, g_ref, b_ref, o32_ref, o16_ref, y1, y2, sem):
    i = pl.program_id(0)
    tm = y1.shape[1]
    slot = i % 2

    def gather(p_ref, s):
        def start(r, _):
            _row_copy(ys_hbm, y1.at[s], p_ref[0, r], r, sem.at[s]).start()
            _row_copy(ys_hbm, y2.at[s], p_ref[0, tm + r], r, sem.at[s]).start()
            return _

        lax.fori_loop(0, tm, start, 0, unroll=8)

    @pl.when(i == 0)
    def _():
        gather(pos_ref, 0)

    @pl.when(i + 1 < pl.num_programs(0))
    def _():
        gather(next_pos_ref, 1 - slot)

    pltpu.make_async_copy(ys_hbm.at[pl.ds(0, tm), :], y1.at[slot], sem.at[slot]).wait()
    pltpu.make_async_copy(ys_hbm.at[pl.ds(0, tm), :], y2.at[slot], sem.at[slot]).wait()
    route = route_ref[...]
    w1 = route[:, R_W1:R_W1 + 1]
    w2 = route[:, R_W2:R_W2 + 1]
    f = w1 * y1[slot] + w2 * y2[slot]
    y = _layer_norm(ALPHA * h_ref[...] + f, g_ref[...], b_ref[...])
    o32_ref[...] = y
    o16_ref[...] = y.astype(BF16)


def _combine_call(pos, ys, route, h32, g, b):
    n, d = h32.shape
    nt, _, two_tm = pos.shape
    tm = two_tm // 2
    row = pl.BlockSpec((tm, d), lambda i: (i, 0))
    vec = pl.BlockSpec((1, d), lambda i: (0, 0))
    return pl.pallas_call(
        _combine_kernel,
        grid=(nt,),
        in_specs=[pl.BlockSpec((None, 1, two_tm), lambda i: (i, 0, 0), memory_space=pltpu.SMEM),
                  pl.BlockSpec((None, 1, two_tm), lambda i: (jnp.minimum(i + 1, nt - 1), 0, 0),
                               memory_space=pltpu.SMEM),
                  pl.BlockSpec(memory_space=pl.ANY),
                  pl.BlockSpec((tm, LANES), lambda i: (i, 0)), row, vec, vec],
        out_specs=[row, row],
        out_shape=[jax.ShapeDtypeStruct((n, d), F32), jax.ShapeDtypeStruct((n, d), BF16)],
        scratch_shapes=[pltpu.VMEM((2, tm, d), F32), pltpu.VMEM((2, tm, d), F32), pltpu.SemaphoreType.DMA((2,))],
        compiler_params=_params("arbitrary"),
        name="moe_combine",
    )(pos, pos, ys, route, h32, g.reshape(1, d), b.reshape(1, d))


def _tiled_positions(pos1, pos2, tm):
    nt = pos1.shape[0] // tm
    return jnp.concatenate([pos1.reshape(nt, 1, tm), pos2.reshape(nt, 1, tm)], axis=2)


def _moe_layer(h32, w_router, wg, wu, wd, g, b):
    n, d = h32.shape
    tm = MOE_TILE
    n_rows = ((2 * n + N_EXPERTS * (tm - 1) + tm - 1) // tm) * tm
    n_tiles = n_rows // tm

    route, counts = _router_call(h32, w_router)
    count = counts[0, :N_EXPERTS].astype(jnp.int32)
    padded = ((count + tm - 1) // tm) * tm
    end = jnp.cumsum(padded)
    start = end - padded
    i1 = route[:, R_I1].astype(jnp.int32)
    i2 = route[:, R_I2].astype(jnp.int32)
    pos1 = start[i1] + route[:, R_RANK1].astype(jnp.int32)
    pos2 = start[i2] + route[:, R_RANK2].astype(jnp.int32)
    n_active = (end[-1] // tm).astype(jnp.int32)
    tile_start = jnp.arange(n_tiles, dtype=jnp.int32) * tm
    tile_expert = jnp.minimum(jnp.sum(tile_start[:, None] >= end[None, :], axis=1), N_EXPERTS - 1).astype(jnp.int32)
    tile_expert = jnp.where(tile_start < end[-1], tile_expert, tile_expert[jnp.maximum(n_active - 1, 0)])

    td = _pick_tile(n, (1280, 640, 512, 256, 128))
    xs = _dispatch_call(_tiled_positions(pos1, pos2, td), h32, n_rows)
    ys = _grouped_ffn_call(tile_expert, n_active.reshape(1), xs, wg, wu, wd)
    tc = _pick_tile(n, (512, 256, 128))
    return _combine_call(_tiled_positions(pos1, pos2, tc), ys, route, h32, g, b)


def _permute_w_in(w):
    sizes = (D_MODEL, D_MODEL, ATTN_WIDTH, KV_WIDTH, KV_WIDTH, D_INNER, D_INNER, BC_WIDTH, BC_WIDTH, SSD_HEADS)
    offs = [0]
    for s in sizes:
        offs.append(offs[-1] + s)
    g_a, g_s, q, k, v, z, xs, bm, cm, dt = (w[:, offs[i]:offs[i + 1]] for i in range(len(sizes)))
    q = q * (HEAD_DIM ** -0.5)
    tail = jnp.zeros((w.shape[0], PROJ_WIDTH - OFF_DT - SSD_HEADS), w.dtype)
    return jnp.concatenate([z, xs, g_a, g_s, q, bm, cm, k, v, dt, tail], axis=1).astype(BF16)


def kernel(x, meta_tokens, ln_in_g, ln_in_b, w_in, conv_w, conv_b, dt_bias, a_log, d_skip, ssd_norm_g, sinks,
           w_attn_out, w_ssd_out, w_o, ln1_g, ln1_b, ffn_wg, ffn_wu, ffn_wd, moe_router, moe_wg, moe_wu, moe_wd,
           ln2_g, ln2_b):
    batch, seq, d = x.shape
    assert d == D_MODEL and seq % BLOCK == 0
    t_len = seq + BLOCK
    nb = t_len // BLOCK
    n = batch * t_len

    h0 = jnp.concatenate([jnp.zeros((batch, PAD, d), x.dtype),
                          jnp.broadcast_to(meta_tokens.astype(x.dtype)[None], (batch, N_META, d)), x], axis=1)
    h32, hb = _ln_call(h0.reshape(n, d), ln_in_g, ln_in_b)

    pos = (jnp.arange(t_len) - PAD).astype(F32)
    inv_freq = ROPE_THETA ** (-jnp.arange(0, HEAD_DIM, 2, dtype=F32) / HEAD_DIM)
    ang = pos[:, None] * inv_freq[None, :]
    cos, sin = jnp.cos(ang), jnp.sin(ang)
    cos_t = jnp.tile(jnp.concatenate([cos, cos], axis=1), (1, LANES // HEAD_DIM))
    sin_t = jnp.tile(jnp.concatenate([-sin, sin], axis=1), (1, LANES // HEAD_DIM))

    lane = jnp.arange(LANES)
    partner = jnp.where(lane % HEAD_DIM < HEAD_DIM // 2, lane + HEAD_DIM // 2, lane - HEAD_DIM // 2)
    perm = (lane[:, None] == partner[None, :]).astype(BF16)

    head_of_lane = jnp.arange(D_INNER) // SSD_HEAD_DIM
    expand = (jnp.arange(LANES)[:, None] == head_of_lane[None, :]).astype(BF16)

    for l in range(DEPTH):
        proj = _proj_call(hb, _permute_w_in(w_in[l]))
        attn = _attn_call(proj, sinks[l], cos_t, sin_t, perm, batch, nb)
        yn = _ssd_call(proj, conv_w[l], conv_b[l], dt_bias[l], a_log[l], d_skip[l], ssd_norm_g[l], expand, batch, nb)
        h32, hb = _merge_call(attn, yn, proj, h32, w_attn_out[l].astype(BF16), w_ssd_out[l].astype(BF16),
                              w_o[l].astype(BF16), ln1_g[l], ln1_b[l])
        i = l // 2
        if l % 2 == 0:
            h32, hb = _ffn_call(hb, h32, ffn_wg[i].astype(BF16), ffn_wu[i].astype(BF16), ffn_wd[i].astype(BF16),
                                ln2_g[l], ln2_b[l])
        else:
            h32, hb = _moe_layer(h32, moe_router[i], moe_wg[i].astype(BF16), moe_wu[i].astype(BF16),
                                 moe_wd[i].astype(BF16), ln2_g[l], ln2_b[l])
    return h32.reshape(batch, t_len, d)[:, BLOCK:]
```

```python
import jax
import jax.numpy as jnp
from jax import lax
from jax.experimental import pallas as pl
from jax.experimental.pallas import tpu as pltpu

D_MODEL = 1024
DEPTH = 4
N_META = 16
BLOCK = 128
PAD = BLOCK - N_META
HEAD_DIM = 64
N_Q_HEADS = 16
N_KV_HEADS = 4
ROPE_THETA = 10000.0
ATTN_WIDTH = N_Q_HEADS * HEAD_DIM
KV_WIDTH = N_KV_HEADS * HEAD_DIM
D_INNER = 2 * D_MODEL
SSD_HEAD_DIM = 64
SSD_HEADS = D_INNER // SSD_HEAD_DIM
SSD_GROUPS = 4
SSD_STATE = 128
GROUP_WIDTH = D_INNER // SSD_GROUPS
CONV_WIDTH = 4
BC_WIDTH = SSD_GROUPS * SSD_STATE
D_FF = 2816
N_EXPERTS = 8
D_FF_EXPERT = 3584
ALPHA = (2 * DEPTH) ** 0.25
LN_EPS = 1e-5
RMS_EPS = 1e-5

LANES = 128
NEG_BIG = -1e30
VMEM_LIMIT = 56 * 1024 * 1024

OFF_Z = 0
OFF_XS = OFF_Z + D_INNER
OFF_GA = OFF_XS + D_INNER
OFF_GS = OFF_GA + D_MODEL
OFF_Q = OFF_GS + D_MODEL
OFF_B = OFF_Q + ATTN_WIDTH
OFF_C = OFF_B + BC_WIDTH
OFF_K = OFF_C + BC_WIDTH
OFF_V = OFF_K + KV_WIDTH
OFF_DT = OFF_V + KV_WIDTH
PROJ_WIDTH = OFF_DT + LANES
PROJ_TILE = PROJ_WIDTH // 3

F32 = jnp.float32
BF16 = jnp.bfloat16


def _pick_tile(n, candidates):
    for c in candidates:
        if n % c == 0:
            return c
    raise ValueError(f"no tile in {candidates} divides {n}")


def _params(*semantics):
    return pltpu.CompilerParams(dimension_semantics=semantics, vmem_limit_bytes=VMEM_LIMIT)


def _layer_norm(x, g, b):
    mu = jnp.mean(x, axis=-1, keepdims=True)
    xc = x - mu
    var = jnp.mean(xc * xc, axis=-1, keepdims=True)
    return xc * lax.rsqrt(var + LN_EPS) * g + b


def _silu(x):
    half = 0.5 * x
    return half + half * jnp.tanh(half)


def _dot(a, b):
    return jnp.dot(a, b, preferred_element_type=F32)


def _dot_nt(a, b):
    return lax.dot_general(a, b, (((1,), (1,)), ((), ())), preferred_element_type=F32)


def _dot_tn(a, b):
    return lax.dot_general(a, b, (((0,), (0,)), ((), ())), preferred_element_type=F32)


def _split2(x):
    hi = x.astype(BF16)
    lo = (x - hi.astype(F32)).astype(BF16)
    return hi, lo


def _dot_exact_rhs(x, m):
    hi, lo = _split2(x)
    return _dot(hi, m) + _dot(lo, m)


def _ln_kernel(x_ref, g_ref, b_ref, o32_ref, o16_ref):
    y = _layer_norm(x_ref[...], g_ref[...], b_ref[...])
    o32_ref[...] = y
    o16_ref[...] = y.astype(BF16)


def _ln_call(x, g, b):
    n, d = x.shape
    tm = _pick_tile(n, (1280, 640, 512, 256, 128))
    row = pl.BlockSpec((tm, d), lambda i: (i, 0))
    vec = pl.BlockSpec((1, d), lambda i: (0, 0))
    return pl.pallas_call(
        _ln_kernel,
        grid=(n // tm,),
        in_specs=[row, vec, vec],
        out_specs=[row, row],
        out_shape=[jax.ShapeDtypeStruct((n, d), F32), jax.ShapeDtypeStruct((n, d), BF16)],
        compiler_params=_params("parallel"),
        name="ln_in",
    )(x, g.reshape(1, d), b.reshape(1, d))


def _proj_kernel(x_ref, w_ref, o_ref):
    o_ref[...] = _dot(x_ref[...], w_ref[...])


def _proj_call(hb, w):
    n, d = hb.shape
    width = w.shape[1]
    tm = _pick_tile(n, (1280, 640, 512, 256, 128))
    tn = PROJ_TILE
    return pl.pallas_call(
        _proj_kernel,
        grid=(n // tm, width // tn),
        in_specs=[pl.BlockSpec((tm, d), lambda i, j: (i, 0)), pl.BlockSpec((d, tn), lambda i, j: (0, j))],
        out_specs=pl.BlockSpec((tm, tn), lambda i, j: (i, j)),
        out_shape=jax.ShapeDtypeStruct((n, width), F32),
        compiler_params=_params("parallel", "arbitrary"),
        name="in_proj",
    )(hb, w)


def _half_split(t):
    lane = lax.broadcasted_iota(jnp.int32, t.shape, 1)
    low = lane < HEAD_DIM
    swapped = pltpu.roll(t, HEAD_DIM, 1)
    zero = jnp.zeros_like(t)
    a_lo = jnp.where(low, t, zero).astype(BF16)
    a_hi = jnp.where(low, zero, swapped).astype(BF16)
    b_lo = jnp.where(low, swapped, zero).astype(BF16)
    b_hi = jnp.where(low, zero, t).astype(BF16)
    return (a_lo, a_hi), (b_lo, b_hi)


def _attn_kernel(sink_ref, q_ref, k_ref, v_ref, cos_ref, sin_ref, perm_ref, o_ref, kop, vop, s_scr, e_scr):
    n = pl.program_id(1)

    @pl.when(n == 0)
    def _():
        kop[...] = jnp.zeros_like(kop)
        vop[...] = jnp.zeros_like(vop)

    cos, sin = cos_ref[...], sin_ref[...]
    perm = perm_ref[...]

    def rope(x):
        return x * cos + _dot(x.astype(BF16), perm) * sin

    for t in range(KV_WIDTH // LANES):
        ks = _half_split(rope(k_ref[:, t * LANES:(t + 1) * LANES]))
        vs = _half_split(v_ref[:, t * LANES:(t + 1) * LANES])
        for which in range(2):
            g = 2 * t + which
            for half in range(2):
                kop[g, half, 0:BLOCK, :] = kop[g, half, BLOCK:2 * BLOCK, :]
                vop[g, half, 0:BLOCK, :] = vop[g, half, BLOCK:2 * BLOCK, :]
                kop[g, half, BLOCK:2 * BLOCK, :] = ks[which][half]
                vop[g, half, BLOCK:2 * BLOCK, :] = vs[which][half]

    tiles_per_kv = (N_Q_HEADS // N_KV_HEADS) * HEAD_DIM // LANES
    jj = lax.broadcasted_iota(jnp.int32, (2 * BLOCK, BLOCK), 0)
    i = lax.broadcasted_iota(jnp.int32, (2 * BLOCK, BLOCK), 1)
    visible = (jj > i) & (jj <= i + BLOCK) & (jj + (n - 1) * BLOCK >= PAD)
    bias = jnp.tile(jnp.where(visible, 0.0, NEG_BIG).astype(F32), (1, tiles_per_kv))

    for g in range(N_KV_HEADS):
        t0 = g * tiles_per_kv
        qg = jnp.concatenate(
            [rope(q_ref[:, (t0 + t) * LANES:(t0 + t + 1) * LANES]).astype(BF16) for t in range(tiles_per_kv)], axis=0)
        for half in range(2):
            s_scr[2 * g + half] = _dot_nt(kop[g, half], qg)
    inv_denom = []
    for g in range(N_KV_HEADS):
        t0 = g * tiles_per_kv
        for half in range(2):
            s = s_scr[2 * g + half] + bias
            sink = jnp.concatenate(
                [jnp.full((1, BLOCK), sink_ref[2 * (t0 + t) + half], F32) for t in range(tiles_per_kv)], axis=1)
            m = jnp.maximum(jnp.max(s, axis=0, keepdims=True), sink)
            e = jnp.exp(s - m)
            inv_denom.append(1.0 / (jnp.sum(e, axis=0, keepdims=True) + jnp.exp(sink - m)))
            e_scr[2 * g + half] = e.astype(BF16)
    for g in range(N_KV_HEADS):
        t0 = g * tiles_per_kv
        acc = (_dot_tn(vop[g, 0], e_scr[2 * g]) * inv_denom[2 * g]
               + _dot_tn(vop[g, 1], e_scr[2 * g + 1]) * inv_denom[2 * g + 1])
        for t in range(tiles_per_kv):
            o_ref[(t0 + t) * LANES:(t0 + t + 1) * LANES, :] = acc[:, t * BLOCK:(t + 1) * BLOCK].astype(o_ref.dtype)


def _attn_call(proj, sinks, cos_t, sin_t, perm, batch, nb):
    n = proj.shape[0]

    def row(b, i):
        return b * nb + i

    kv_blk = OFF_K // KV_WIDTH
    in_specs = [
        pl.BlockSpec(memory_space=pltpu.SMEM),
        pl.BlockSpec((BLOCK, ATTN_WIDTH), lambda b, i: (row(b, i), OFF_Q // ATTN_WIDTH)),
        pl.BlockSpec((BLOCK, KV_WIDTH), lambda b, i: (row(b, i), kv_blk)),
        pl.BlockSpec((BLOCK, KV_WIDTH), lambda b, i: (row(b, i), kv_blk + 1)),
        pl.BlockSpec((BLOCK, LANES), lambda b, i: (i, 0)),
        pl.BlockSpec((BLOCK, LANES), lambda b, i: (i, 0)),
        pl.BlockSpec((LANES, LANES), lambda b, i: (0, 0)),
    ]
    operand = pltpu.VMEM((N_KV_HEADS, 2, 2 * BLOCK, LANES), BF16)
    return pl.pallas_call(
        _attn_kernel,
        grid=(batch, nb),
        in_specs=in_specs,
        out_specs=pl.BlockSpec((ATTN_WIDTH, BLOCK), lambda b, i: (0, row(b, i))),
        out_shape=jax.ShapeDtypeStruct((ATTN_WIDTH, n), BF16),
        scratch_shapes=[operand, operand,
                        pltpu.VMEM((2 * N_KV_HEADS, 2 * BLOCK, ATTN_WIDTH // N_KV_HEADS), F32),
                        pltpu.VMEM((2 * N_KV_HEADS, 2 * BLOCK, ATTN_WIDTH // N_KV_HEADS), BF16)],
        compiler_params=_params("parallel", "arbitrary"),
        name="swa_attn",
    )(sinks, proj, proj, proj, cos_t, sin_t, perm)


CARRY = 8


def _causal_conv(buf_ref, raw, w_ref, b_ref):
    buf_ref[CARRY:CARRY + BLOCK, :] = raw
    acc = b_ref[...] + w_ref[CONV_WIDTH - 1:CONV_WIDTH, :] * raw
    for j in range(CONV_WIDTH - 1):
        start = CARRY - (CONV_WIDTH - 1) + j
        acc = acc + w_ref[j:j + 1, :] * buf_ref[start:start + BLOCK, :]
    buf_ref[0:CARRY, :] = raw[BLOCK - CARRY:, :]
    return acc


def _ssd_kernel(xs_ref, z_ref, b_ref, c_ref, dt_ref, cwx_ref, cwb_ref, cwc_ref, cbx_ref, cbb_ref, cbc_ref,
                dtb_ref, alog_ref, dskip_ref, ng_ref, exp_ref, o_ref,
                bufx, bufb, bufc, state, y_scr):
    c = pl.program_id(1)

    @pl.when(c == 0)
    def _():
        state[...] = jnp.zeros_like(state)
        bufx[0:CARRY, :] = jnp.zeros((CARRY, D_INNER), F32)
        bufb[0:CARRY, :] = jnp.zeros((CARRY, BC_WIDTH), F32)
        bufc[0:CARRY, :] = jnp.zeros((CARRY, BC_WIDTH), F32)

    row = lax.broadcasted_iota(jnp.int32, (BLOCK, 1), 0)
    valid = ((row + c * BLOCK) >= PAD).astype(F32)

    xs = _silu(_causal_conv(bufx, xs_ref[...] * valid, cwx_ref, cbx_ref)) * valid
    bm = _silu(_causal_conv(bufb, b_ref[...] * valid, cwb_ref, cbb_ref)).astype(BF16)
    cm = _silu(_causal_conv(bufc, c_ref[...] * valid, cwc_ref, cbc_ref)).astype(BF16)

    dt = jnp.logaddexp(dt_ref[...] + dtb_ref[...], 0.0)
    da = dt * (-jnp.exp(alog_ref[...]))
    li = lax.broadcasted_iota(jnp.int32, (BLOCK, BLOCK), 0)
    si = lax.broadcasted_iota(jnp.int32, (BLOCK, BLOCK), 1)
    causal = li >= si
    tril = jnp.where(causal, 1.0, 0.0).astype(BF16)
    hi = da.astype(BF16)
    r1 = da - hi.astype(F32)
    mid = r1.astype(BF16)
    lo = (r1 - mid.astype(F32)).astype(BF16)
    a_cum = _dot(tril, hi) + _dot(tril, mid) + _dot(tril, lo)
    a_cum_t = a_cum.T
    a_last = a_cum[BLOCK - 1:BLOCK, :]
    exp_a = jnp.exp(a_cum)
    decay_to_end = jnp.exp(a_last - a_cum)

    expand = exp_ref[...]
    dt_x = _dot_exact_rhs(dt, expand)
    exp_a_x = _dot_exact_rhs(exp_a, expand)
    decay_x = _dot_exact_rhs(decay_to_end, expand)
    chunk_decay_x = exp_a_x[BLOCK - 1:BLOCK, :]

    xdt = xs * dt_x
    xdec_b = (xdt * decay_x).astype(BF16)

    lane = lax.broadcasted_iota(jnp.int32, (BLOCK, LANES), 1)
    low = lane < SSD_HEAD_DIM
    heads_per_group = SSD_HEADS // SSD_GROUPS
    for g in range(SSD_GROUPS):
        gs = slice(g * GROUP_WIDTH, (g + 1) * GROUP_WIDTH)
        bg = bm[:, g * SSD_STATE:(g + 1) * SSD_STATE]
        cg = cm[:, g * SSD_STATE:(g + 1) * SSD_STATE]
        cb = _dot_nt(cg, bg)
        bg_t = bg.astype(F32).T.astype(BF16)
        new_states = _dot(bg_t, xdec_b[:, gs])
        st = state[:, gs]
        y_off = _dot(cg, st.astype(BF16)) * exp_a_x[:, gs]
        state[:, gs] = st * chunk_decay_x[:, gs] + new_states
        for j in range(heads_per_group // 2):
            h0 = g * heads_per_group + 2 * j
            ms = []
            for h in (h0, h0 + 1):
                seg = a_cum[:, h:h + 1] - a_cum_t[h:h + 1, :]
                ms.append((cb * jnp.exp(jnp.where(causal, seg, NEG_BIG))).astype(BF16))
            lhs = jnp.concatenate(ms, axis=1)
            xp = xdt[:, h0 * SSD_HEAD_DIM:h0 * SSD_HEAD_DIM + LANES]
            zero = jnp.zeros_like(xp)
            rhs = jnp.concatenate([jnp.where(low, xp, zero), jnp.where(low, zero, xp)], axis=0).astype(BF16)
            y_scr[:, h0 * SSD_HEAD_DIM:h0 * SSD_HEAD_DIM + LANES] = (
                _dot(lhs, rhs) + y_off[:, 2 * j * SSD_HEAD_DIM:2 * j * SSD_HEAD_DIM + LANES])

    y = y_scr[...] + xs * dskip_ref[...]
    u = y * _silu(z_ref[...])
    for g in range(SSD_GROUPS):
        gs = slice(g * GROUP_WIDTH, (g + 1) * GROUP_WIDTH)
        ug = u[:, gs]
        scale = lax.rsqrt(jnp.mean(ug * ug, axis=-1, keepdims=True) + RMS_EPS)
        o_ref[:, gs] = (ug * scale * ng_ref[:, gs]).astype(o_ref.dtype)


def _ssd_call(proj, conv_w, conv_b, dt_bias, a_log, d_skip, norm_g, expand, batch, nc):
    n = proj.shape[0]

    def row(b, i):
        return b * nc + i

    def full(shape):
        return pl.BlockSpec(shape, lambda b, i: (0,) * len(shape))

    cwx, cwb, cwc = conv_w[:, :D_INNER], conv_w[:, D_INNER:D_INNER + BC_WIDTH], conv_w[:, D_INNER + BC_WIDTH:]
    cbx = conv_b[None, :D_INNER]
    cbb = conv_b[None, D_INNER:D_INNER + BC_WIDTH]
    cbc = conv_b[None, D_INNER + BC_WIDTH:]
    pad = LANES - SSD_HEADS
    dtb = jnp.pad(dt_bias, (0, pad))[None, :]
    alog = jnp.pad(a_log, (0, pad))[None, :]
    dskip = jnp.repeat(d_skip, SSD_HEAD_DIM)[None, :]
    in_specs = [
        pl.BlockSpec((BLOCK, D_INNER), lambda b, i: (row(b, i), OFF_XS // D_INNER)),
        pl.BlockSpec((BLOCK, D_INNER), lambda b, i: (row(b, i), OFF_Z // D_INNER)),
        pl.BlockSpec((BLOCK, BC_WIDTH), lambda b, i: (row(b, i), OFF_B // BC_WIDTH)),
        pl.BlockSpec((BLOCK, BC_WIDTH), lambda b, i: (row(b, i), OFF_C // BC_WIDTH)),
        pl.BlockSpec((BLOCK, LANES), lambda b, i: (row(b, i), OFF_DT // LANES)),
        full((CONV_WIDTH, D_INNER)), full((CONV_WIDTH, BC_WIDTH)), full((CONV_WIDTH, BC_WIDTH)),
        full((1, D_INNER)), full((1, BC_WIDTH)), full((1, BC_WIDTH)),
        full((1, LANES)), full((1, LANES)), full((1, D_INNER)), full((1, D_INNER)),
        full((LANES, D_INNER)),
    ]
    scratch = [
        pltpu.VMEM((CARRY + BLOCK, D_INNER), F32),
        pltpu.VMEM((CARRY + BLOCK, BC_WIDTH), F32),
        pltpu.VMEM((CARRY + BLOCK, BC_WIDTH), F32),
        pltpu.VMEM((SSD_STATE, D_INNER), F32),
        pltpu.VMEM((BLOCK, D_INNER), F32),
    ]
    return pl.pallas_call(
        _ssd_kernel,
        grid=(batch, nc),
        in_specs=in_specs,
        out_specs=pl.BlockSpec((BLOCK, D_INNER), lambda b, i: (row(b, i), 0)),
        out_shape=jax.ShapeDtypeStruct((n, D_INNER), BF16),
        scratch_shapes=scratch,
        compiler_params=_params("arbitrary", "arbitrary"),
        name="ssd",
    )(proj, proj, proj, proj, proj, cwx, cwb, cwc, cbx, cbb, cbc, dtb, alog, dskip, norm_g[None, :], expand)


def _merge_kernel(attn_ref, y_ref, ga_ref, gs_ref, h_ref, wa_ref, ws_ref, wo_ref, g_ref, b_ref, o32_ref, o16_ref):
    a = _dot_tn(attn_ref[...], wa_ref[...])
    s = _dot(y_ref[...], ws_ref[...])
    merged = jax.nn.sigmoid(ga_ref[...]) * a + jax.nn.sigmoid(gs_ref[...]) * s
    mix = _dot(merged.astype(BF16), wo_ref[...])
    y = _layer_norm(ALPHA * h_ref[...] + mix, g_ref[...], b_ref[...])
    o32_ref[...] = y
    o16_ref[...] = y.astype(BF16)


def _merge_call(attn, yn, proj, h32, wa, ws, wo, g, b):
    n, d = h32.shape
    tm = _pick_tile(n, (640, 512, 256, 128))
    row = pl.BlockSpec((tm, d), lambda i: (i, 0))
    vec = pl.BlockSpec((1, d), lambda i: (0, 0))

    def const(shape):
        return pl.BlockSpec(shape, lambda i: (0, 0), pipeline_mode=pl.Buffered(1))

    in_specs = [
        pl.BlockSpec((ATTN_WIDTH, tm), lambda i: (0, i)),
        pl.BlockSpec((tm, D_INNER), lambda i: (i, 0)),
        pl.BlockSpec((tm, D_MODEL), lambda i: (i, OFF_GA // D_MODEL)),
        pl.BlockSpec((tm, D_MODEL), lambda i: (i, OFF_GS // D_MODEL)),
        row,
        const((ATTN_WIDTH, d)), const((D_INNER, d)), const((d, d)),
        vec, vec,
    ]
    return pl.pallas_call(
        _merge_kernel,
        grid=(n // tm,),
        in_specs=in_specs,
        out_specs=[row, row],
        out_shape=[jax.ShapeDtypeStruct((n, d), F32), jax.ShapeDtypeStruct((n, d), BF16)],
        compiler_params=_params("parallel"),
        name="merge_out",
    )(attn, yn, proj, proj, h32, wa, ws, wo, g.reshape(1, d), b.reshape(1, d))


FF_CHUNK = 256


def _swiglu(xb, wg_ref, wu_ref, wd_ref):
    f = wg_ref.shape[1]
    acc = jnp.zeros((xb.shape[0], wd_ref.shape[1]), F32)
    for c in range(f // FF_CHUNK):
        cs = slice(c * FF_CHUNK, (c + 1) * FF_CHUNK)
        act = (_silu(_dot(xb, wg_ref[:, cs])) * _dot(xb, wu_ref[:, cs])).astype(BF16)
        acc = acc + _dot(act, wd_ref[cs, :])
    return acc


def _ffn_kernel(hb_ref, h_ref, wg_ref, wu_ref, wd_ref, g_ref, b_ref, o32_ref, o16_ref):
    f = _swiglu(hb_ref[...], wg_ref, wu_ref, wd_ref)
    y = _layer_norm(ALPHA * h_ref[...] + f, g_ref[...], b_ref[...])
    o32_ref[...] = y
    o16_ref[...] = y.astype(BF16)


def _ffn_call(hb, h32, wg, wu, wd, g, b):
    n, d = h32.shape
    f = wg.shape[1]
    assert f % FF_CHUNK == 0
    tm = _pick_tile(n, (640, 512, 256, 128))
    row = pl.BlockSpec((tm, d), lambda i: (i, 0))
    vec = pl.BlockSpec((1, d), lambda i: (0, 0))

    def const(shape):
        return pl.BlockSpec(shape, lambda i: (0, 0), pipeline_mode=pl.Buffered(1))

    return pl.pallas_call(
        _ffn_kernel,
        grid=(n // tm,),
        in_specs=[row, row, const((d, f)), const((d, f)), const((f, d)), vec, vec],
        out_specs=[row, row],
        out_shape=[jax.ShapeDtypeStruct((n, d), F32), jax.ShapeDtypeStruct((n, d), BF16)],
        compiler_params=_params("parallel"),
        name="dense_ffn",
    )(hb, h32, wg, wu, wd, g.reshape(1, d), b.reshape(1, d))


R_I1, R_I2, R_W1, R_W2, R_RANK1, R_RANK2 = range(6)


def _router_kernel(h_ref, w_ref, route_ref, count_ref, carry):
    @pl.when(pl.program_id(0) == 0)
    def _():
        carry[...] = jnp.zeros_like(carry)

    h = h_ref[...]
    tm = h.shape[0]
    lane = lax.broadcasted_iota(jnp.int32, (tm, LANES), 1)
    logits = jnp.full((tm, LANES), NEG_BIG, F32)
    for e in range(N_EXPERTS):
        logits = jnp.where(lane == e, jnp.sum(h * w_ref[e:e + 1, :], axis=-1, keepdims=True), logits)
    v1 = jnp.max(logits, axis=-1, keepdims=True)
    i1 = jnp.min(jnp.where(logits == v1, lane, LANES), axis=-1, keepdims=True)
    rest = jnp.where(lane == i1, NEG_BIG, logits)
    v2 = jnp.max(rest, axis=-1, keepdims=True)
    i2 = jnp.min(jnp.where(rest == v2, lane, LANES), axis=-1, keepdims=True)
    e2 = jnp.exp(v2 - v1)
    w1 = 1.0 / (1.0 + e2)
    w2 = e2 / (1.0 + e2)

    hot1 = jnp.where(lane == i1, 1.0, 0.0)
    hot2 = jnp.where(lane == i2, 1.0, 0.0)
    ri = lax.broadcasted_iota(jnp.int32, (tm, tm), 0)
    ci = lax.broadcasted_iota(jnp.int32, (tm, tm), 1)
    before = jnp.where(ri > ci, 1.0, 0.0).astype(BF16)
    base = carry[0:1, :]
    count1 = jnp.sum(hot1, axis=0, keepdims=True)
    rank1 = jnp.sum(hot1 * (_dot(before, hot1.astype(BF16)) + base), axis=-1, keepdims=True)
    rank2 = jnp.sum(hot2 * (_dot(before, hot2.astype(BF16)) + base + count1), axis=-1, keepdims=True)
    total = base + count1 + jnp.sum(hot2, axis=0, keepdims=True)
    carry[0:1, :] = total

    rec = jnp.zeros(logits.shape, F32)
    for slot, val in ((R_I1, i1.astype(F32)), (R_I2, i2.astype(F32)), (R_W1, w1), (R_W2, w2),
                      (R_RANK1, rank1), (R_RANK2, rank2)):
        rec = jnp.where(lane == slot, val, rec)
    route_ref[...] = rec
    count_ref[...] = jnp.broadcast_to(total, count_ref.shape)


def _router_call(h32, w_router):
    n, d = h32.shape
    tm = _pick_tile(n, (1280, 640, 512, 256, 128))
    w = w_router.T
    return pl.pallas_call(
        _router_kernel,
        grid=(n // tm,),
        in_specs=[pl.BlockSpec((tm, d), lambda i: (i, 0)), pl.BlockSpec((N_EXPERTS, d), lambda i: (0, 0))],
        out_specs=[pl.BlockSpec((tm, LANES), lambda i: (i, 0)), pl.BlockSpec((8, LANES), lambda i: (0, 0))],
        out_shape=[jax.ShapeDtypeStruct((n, LANES), F32), jax.ShapeDtypeStruct((8, LANES), F32)],
        scratch_shapes=[pltpu.VMEM((8, LANES), F32)],
        compiler_params=_params("arbitrary"),
        name="router",
    )(h32, w)


MOE_TILE = 512


def _row_copy(src, dst, src_row, dst_row, sem):
    return pltpu.make_async_copy(src.at[pl.ds(src_row, 1), :], dst.at[pl.ds(dst_row, 1), :], sem)


def _dispatch_kernel(pos_ref, h_ref, init_hbm, xs_hbm, sem):
    del init_hbm
    tm = h_ref.shape[0]

    def start(r, _):
        _row_copy(h_ref, xs_hbm, r, pos_ref[0, r], sem).start(priority=0)
        _row_copy(h_ref, xs_hbm, r, pos_ref[0, tm + r], sem).start(priority=1)
        return _

    lax.fori_loop(0, tm, start, 0, unroll=8)
    for _ in range(2):
        pltpu.make_async_copy(h_ref, xs_hbm.at[pl.ds(0, tm), :], sem).wait()


def _dispatch_call(pos, h32, n_rows):
    n, d = h32.shape
    nt, _, two_tm = pos.shape
    return pl.pallas_call(
        _dispatch_kernel,
        grid=(nt,),
        in_specs=[pl.BlockSpec((None, 1, two_tm), lambda i: (i, 0, 0), memory_space=pltpu.SMEM),
                  pl.BlockSpec((two_tm // 2, d), lambda i: (i, 0)), pl.BlockSpec(memory_space=pl.ANY)],
        out_specs=pl.BlockSpec(memory_space=pl.ANY),
        out_shape=jax.ShapeDtypeStruct((n_rows, d), F32),
        scratch_shapes=[pltpu.SemaphoreType.DMA(())],
        input_output_aliases={2: 0},
        compiler_params=_params("arbitrary"),
        name="moe_dispatch",
    )(pos, h32, jnp.zeros((n_rows, d), F32))


def _grouped_ffn_kernel(te_ref, na_ref, x_ref, wg_ref, wu_ref, wd_ref, o_ref):
    del te_ref
    i = pl.program_id(0)

    @pl.when(i < na_ref[0])
    def _():
        o_ref[...] = _swiglu(x_ref[...].astype(BF16), wg_ref, wu_ref, wd_ref)

    @pl.when(i >= na_ref[0])
    def _():
        o_ref[...] = jnp.zeros_like(o_ref)


def _grouped_ffn_call(tile_expert, n_active, xs, wg, wu, wd):
    n_rows, d = xs.shape
    f = wg.shape[2]
    tm = MOE_TILE
    grid_spec = pltpu.PrefetchScalarGridSpec(
        num_scalar_prefetch=2,
        grid=(n_rows // tm,),
        in_specs=[
            pl.BlockSpec((tm, d), lambda i, te, na: (i, 0)),
            pl.BlockSpec((None, d, f), lambda i, te, na: (te[i], 0, 0), pipeline_mode=pl.Buffered(1)),
            pl.BlockSpec((None, d, f), lambda i, te, na: (te[i], 0, 0), pipeline_mode=pl.Buffered(1)),
            pl.BlockSpec((None, f, d), lambda i, te, na: (te[i], 0, 0), pipeline_mode=pl.Buffered(1)),
        ],
        out_specs=pl.BlockSpec((tm, d), lambda i, te, na: (i, 0)),
    )
    return pl.pallas_call(
        _grouped_ffn_kernel,
        grid_spec=grid_spec,
        out_shape=jax.ShapeDtypeStruct((n_rows, d), F32),
        compiler_params=_params("arbitrary"),
        name="moe_grouped_ffn",
    )(tile_expert, n_active, xs, wg, wu, wd)


def _combine_kernel(pos_ref, next_pos_ref, ys_hbm, route_ref, h_ref, g_ref, b_ref, o32_ref, o16_ref, y1, y2, sem):
    i = pl.program_id(0)
    tm = y1.shape[1]
    slot = i % 2

    def gather(p_ref, s):
        def start(r, _):
            _row_copy(ys_hbm, y1.at[s], p_ref[0, r], r, sem.at[s]).start(priority=0)
            _row_copy(ys_hbm, y2.at[s], p_ref[0, tm + r], r, sem.at[s]).start(priority=1)
            return _

        lax.fori_loop(0, tm, start, 0, unroll=8)

    @pl.when(i == 0)
    def _():
        gather(pos_ref, 0)

    @pl.when(i + 1 < pl.num_programs(0))
    def _():
        gather(next_pos_ref, 1 - slot)

    pltpu.make_async_copy(ys_hbm.at[pl.ds(0, tm), :], y1.at[slot], sem.at[slot]).wait()
    pltpu.make_async_copy(ys_hbm.at[pl.ds(0, tm), :], y2.at[slot], sem.at[slot]).wait()
    route = route_ref[...]
    w1 = route[:, R_W1:R_W1 + 1]
    w2 = route[:, R_W2:R_W2 + 1]
    f = w1 * y1[slot] + w2 * y2[slot]
    y = _layer_norm(ALPHA * h_ref[...] + f, g_ref[...], b_ref[...])
    o32_ref[...] = y
    o16_ref[...] = y.astype(BF16)


def _combine_call(pos, ys, route, h32, g, b):
    n, d = h32.shape
    nt, _, two_tm = pos.shape
    tm = two_tm // 2
    row = pl.BlockSpec((tm, d), lambda i: (i, 0))
    vec = pl.BlockSpec((1, d), lambda i: (0, 0))
    return pl.pallas_call(
        _combine_kernel,
        grid=(nt,),
        in_specs=[pl.BlockSpec((None, 1, two_tm), lambda i: (i, 0, 0), memory_space=pltpu.SMEM),
                  pl.BlockSpec((None, 1, two_tm), lambda i: (jnp.minimum(i + 1, nt - 1), 0, 0),
                               memory_space=pltpu.SMEM),
                  pl.BlockSpec(memory_space=pl.ANY),
                  pl.BlockSpec((tm, LANES), lambda i: (i, 0)), row, vec, vec],
        out_specs=[row, row],
        out_shape=[jax.ShapeDtypeStruct((n, d), F32), jax.ShapeDtypeStruct((n, d), BF16)],
        scratch_shapes=[pltpu.VMEM((2, tm, d), F32), pltpu.VMEM((2, tm, d), F32), pltpu.SemaphoreType.DMA((2,))],
        compiler_params=_params("arbitrary"),
        name="moe_combine",
    )(pos, pos, ys, route, h32, g.reshape(1, d), b.reshape(1, d))


def _tiled_positions(pos1, pos2, tm):
    nt = pos1.shape[0] // tm
    return jnp.concatenate([pos1.reshape(nt, 1, tm), pos2.reshape(nt, 1, tm)], axis=2)


def _moe_layer(h32, w_router, wg, wu, wd, g, b):
    n, d = h32.shape
    tm = MOE_TILE
    n_rows = ((2 * n + N_EXPERTS * (tm - 1) + tm - 1) // tm) * tm
    n_tiles = n_rows // tm

    route, counts = _router_call(h32, w_router)
    count = counts[0, :N_EXPERTS].astype(jnp.int32)
    padded = ((count + tm - 1) // tm) * tm
    end = jnp.cumsum(padded)
    start = end - padded
    i1 = route[:, R_I1].astype(jnp.int32)
    i2 = route[:, R_I2].astype(jnp.int32)
    pos1 = start[i1] + route[:, R_RANK1].astype(jnp.int32)
    pos2 = start[i2] + route[:, R_RANK2].astype(jnp.int32)
    n_active = (end[-1] // tm).astype(jnp.int32)
    tile_start = jnp.arange(n_tiles, dtype=jnp.int32) * tm
    tile_expert = jnp.minimum(jnp.sum(tile_start[:, None] >= end[None, :], axis=1), N_EXPERTS - 1).astype(jnp.int32)
    tile_expert = jnp.where(tile_start < end[-1], tile_expert, tile_expert[jnp.maximum(n_active - 1, 0)])

    td = _pick_tile(n, (1280, 640, 512, 256, 128))
    xs = _dispatch_call(_tiled_positions(pos1, pos2, td), h32, n_rows)
    ys = _grouped_ffn_call(tile_expert, n_active.reshape(1), xs, wg, wu, wd)
    tc = _pick_tile(n, (512, 256, 128))
    return _combine_call(_tiled_positions(pos1, pos2, tc), ys, route, h32, g, b)


def _permute_w_in(w):
    sizes = (D_MODEL, D_MODEL, ATTN_WIDTH, KV_WIDTH, KV_WIDTH, D_INNER, D_INNER, BC_WIDTH, BC_WIDTH, SSD_HEADS)
    offs = [0]
    for s in sizes:
        offs.append(offs[-1] + s)
    g_a, g_s, q, k, v, z, xs, bm, cm, dt = (w[:, offs[i]:offs[i + 1]] for i in range(len(sizes)))
    q = q * (HEAD_DIM ** -0.5)
    tail = jnp.zeros((w.shape[0], PROJ_WIDTH - OFF_DT - SSD_HEADS), w.dtype)
    return jnp.concatenate([z, xs, g_a, g_s, q, bm, cm, k, v, dt, tail], axis=1).astype(BF16)


def kernel(x, meta_tokens, ln_in_g, ln_in_b, w_in, conv_w, conv_b, dt_bias, a_log, d_skip, ssd_norm_g, sinks,
           w_attn_out, w_ssd_out, w_o, ln1_g, ln1_b, ffn_wg, ffn_wu, ffn_wd, moe_router, moe_wg, moe_wu, moe_wd,
           ln2_g, ln2_b):
    batch, seq, d = x.shape
    assert d == D_MODEL and seq % BLOCK == 0
    t_len = seq + BLOCK
    nb = t_len // BLOCK
    n = batch * t_len

    h0 = jnp.concatenate([jnp.zeros((batch, PAD, d), x.dtype),
                          jnp.broadcast_to(meta_tokens.astype(x.dtype)[None], (batch, N_META, d)), x], axis=1)
    h32, hb = _ln_call(h0.reshape(n, d), ln_in_g, ln_in_b)

    pos = (jnp.arange(t_len) - PAD).astype(F32)
    inv_freq = ROPE_THETA ** (-jnp.arange(0, HEAD_DIM, 2, dtype=F32) / HEAD_DIM)
    ang = pos[:, None] * inv_freq[None, :]
    cos, sin = jnp.cos(ang), jnp.sin(ang)
    cos_t = jnp.tile(jnp.concatenate([cos, cos], axis=1), (1, LANES // HEAD_DIM))
    sin_t = jnp.tile(jnp.concatenate([-sin, sin], axis=1), (1, LANES // HEAD_DIM))

    lane = jnp.arange(LANES)
    partner = jnp.where(lane % HEAD_DIM < HEAD_DIM // 2, lane + HEAD_DIM // 2, lane - HEAD_DIM // 2)
    perm = (lane[:, None] == partner[None, :]).astype(BF16)

    head_of_lane = jnp.arange(D_INNER) // SSD_HEAD_DIM
    expand = (jnp.arange(LANES)[:, None] == head_of_lane[None, :]).astype(BF16)

    for l in range(DEPTH):
        proj = _proj_call(hb, _permute_w_in(w_in[l]))
        attn = _attn_call(proj, sinks[l], cos_t, sin_t, perm, batch, nb)
        yn = _ssd_call(proj, conv_w[l], conv_b[l], dt_bias[l], a_log[l], d_skip[l], ssd_norm_g[l], expand, batch, nb)
        h32, hb = _merge_call(attn, yn, proj, h32, w_attn_out[l].astype(BF16), w_ssd_out[l].astype(BF16),
                              w_o[l].astype(BF16), ln1_g[l], ln1_b[l])
        i = l // 2
        if l % 2 == 0:
            h32, hb = _ffn_call(hb, h32, ffn_wg[i].astype(BF16), ffn_wu[i].astype(BF16), ffn_wd[i].astype(BF16),
                                ln2_g[l], ln2_b[l])
        else:
            h32, hb = _moe_layer(h32, moe_router[i], moe_wg[i].astype(BF16), moe_wu[i].astype(BF16),
                                 moe_wd[i].astype(BF16), ln2_g[l], ln2_b[l])
    return h32.reshape(batch, t_len, d)[:, BLOCK:]
```
